```python
import math
import jax, jax.numpy as jnp
from jax import lax
import numpy as np

D_MODEL = 2048
BATCH = 8
SEQ = 4096
DEPTH = 2

GRID_W = 64
CTX_LEN = 256
Q_BLOCK = 128
ROPE_THETA = 10000.0
EPS = 1e-6
ALPHA = (2 * DEPTH) ** 0.25
BETA = (8 * DEPTH) ** -0.25
N_EVEN = (DEPTH + 1) // 2
N_ODD = DEPTH // 2

CONV_WIDTH = D_MODEL // 2
CONV_SIZE = 3
MLA_HEADS = D_MODEL // 256
MLA_Q_RANK = 512
MLA_KV_RANK = 512
MLA_NOPE = 128
MLA_ROPE = 64
MLA_V = 128
MIX_EVEN = CONV_WIDTH + MLA_HEADS * MLA_V
EVEN_IN = 3 * CONV_WIDTH + MLA_Q_RANK + MLA_KV_RANK + MLA_ROPE + MIX_EVEN
GQA_HEAD_DIM = 128
GQA_HEADS = D_MODEL // GQA_HEAD_DIM
GQA_KV_HEADS = GQA_HEADS // 4
GQA_GROUP = GQA_HEADS // GQA_KV_HEADS
MIX_ODD = GQA_HEADS * GQA_HEAD_DIM
ODD_IN = MIX_ODD + 2 * GQA_KV_HEADS * GQA_HEAD_DIM + MIX_ODD

kernel_name = 'hybrid_conv_mla_gqa_prefix_deepnorm'


def _rms(x, g):
    xf = x.astype(jnp.float32)
    y = xf * lax.rsqrt(jnp.mean(xf * xf, axis=-1, keepdims=True) + EPS)
    return (y * g.astype(jnp.float32)).astype(x.dtype)


def _post_norm(x, y, g, b):
    z = ALPHA * x.astype(jnp.float32) + y.astype(jnp.float32)
    mu = jnp.mean(z, axis=-1, keepdims=True)
    var = jnp.mean(jnp.square(z - mu), axis=-1, keepdims=True)
    zn = (z - mu) * lax.rsqrt(var + EPS)
    return (zn * g.astype(jnp.float32) + b.astype(jnp.float32)).astype(x.dtype)


def _rope_1d(x, pos):
    half = x.shape[-1] // 2
    inv = ROPE_THETA ** (-jnp.arange(half, dtype=jnp.float32) / half)
    ang = pos[:, None] * inv[None, :]
    cos = jnp.cos(ang)[:, None, :]
    sin = jnp.sin(ang)[:, None, :]
    xf = x.astype(jnp.float32)
    x1, x2 = xf[..., :half], xf[..., half:]
    return jnp.concatenate([x1 * cos - x2 * sin, x1 * sin + x2 * cos], axis=-1).astype(x.dtype)


def _rope_2d(x, pos_row, pos_col):
    d = x.shape[-1] // 2
    return jnp.concatenate([_rope_1d(x[..., :d], pos_row), _rope_1d(x[..., d:], pos_col)], axis=-1)


def _short_conv(u, w):
    up = jnp.pad(u, ((0, 0), (1, 1), (0, 0)))
    return up[:, :-2] * w[0] + up[:, 1:-1] * w[1] + up[:, 2:] * w[2]


def _attend(q, k, v):
    bsz, n, hk, g, dq = q.shape
    nb = n // Q_BLOCK
    scale = dq ** -0.5
    qb = q.reshape(bsz, nb, Q_BLOCK, hk, g, dq).transpose(1, 0, 2, 3, 4, 5)

    def one_block(qi):
        s = jnp.einsum('bqhgd,bkhd->bhgqk', qi, k).astype(jnp.float32) * scale
        p = jax.nn.softmax(s, axis=-1).astype(v.dtype)
        return jnp.einsum('bhgqk,bkhd->bqhgd', p, v)

    o = lax.map(one_block, qb)
    return o.transpose(1, 0, 2, 3, 4, 5).reshape(bsz, n, hk * g * v.shape[-1])


def _even_mixer(h, hc, w_in, conv_w, q_norm, w_qb, kv_norm, w_kvb, w_out, pos, need_ctx):
    o_c = 3 * CONV_WIDTH
    o_q = o_c + MLA_Q_RANK
    o_kv = o_q + MLA_KV_RANK + MLA_ROPE

    def kv_heads(kvp, rpos):
        bsz, m, _ = kvp.shape
        kv = (_rms(kvp[..., :MLA_KV_RANK], kv_norm) @ w_kvb).reshape(bsz, m, MLA_HEADS, MLA_NOPE + MLA_V)
        k_rope = kvp[..., MLA_KV_RANK:][:, :, None, :]
        if rpos is not None:
            k_rope = _rope_2d(k_rope, *rpos)
        k_rope = jnp.broadcast_to(k_rope, (bsz, m, MLA_HEADS, MLA_ROPE))
        return jnp.concatenate([kv[..., :MLA_NOPE], k_rope], axis=-1), kv[..., MLA_NOPE:]

    def q_heads(qp, rpos):
        bsz, m, _ = qp.shape
        q = (_rms(qp, q_norm) @ w_qb).reshape(bsz, m, MLA_HEADS, MLA_NOPE + MLA_ROPE)
        q_rope = q[..., MLA_NOPE:]
        if rpos is not None:
            q_rope = _rope_2d(q_rope, *rpos)
        return jnp.concatenate([q[..., :MLA_NOPE], q_rope], axis=-1)[:, :, :, None, :]

    def branch(p, attn):
        cb = p[..., :CONV_WIDTH]
        cc = p[..., CONV_WIDTH:2 * CONV_WIDTH]
        cu = p[..., 2 * CONV_WIDTH:o_c]
        conv = cb * _short_conv(cc * cu, conv_w)
        gate = jax.nn.silu(p[..., o_kv:])
        return (jnp.concatenate([conv, attn], axis=-1) * gate) @ w_out

    p = h @ w_in
    k_l, v_l = kv_heads(p[..., o_q:o_kv], pos)
    if need_ctx:
        pc = hc @ w_in
        kvc = pc[..., o_q:o_kv]
    else:
        kvc = hc @ w_in[:, o_q:o_kv]
    k_c, v_c = kv_heads(kvc, None)
    k = jnp.concatenate([k_l, k_c], axis=1)
    v = jnp.concatenate([v_l, v_c], axis=1)
    y = branch(p, _attend(q_heads(p[..., o_c:o_q], pos), k, v))
    y_ctx = branch(pc, _attend(q_heads(pc[..., o_c:o_q], None), k_c, v_c)) if need_ctx else None
    return y, y_ctx


def _odd_mixer(h, hc, w_in, q_norm, k_norm, w_out, pos, need_ctx):
    o_q = MIX_ODD
    o_k = o_q + GQA_KV_HEADS * GQA_HEAD_DIM
    o_v = o_k + GQA_KV_HEADS * GQA_HEAD_DIM

    def kv_of(kvp, rpos):
        bsz, m, _ = kvp.shape
        k = _rms(kvp[..., :o_k - o_q].reshape(bsz, m, GQA_KV_HEADS, GQA_HEAD_DIM), k_norm)
        v = kvp[..., o_k - o_q:].reshape(bsz, m, GQA_KV_HEADS, GQA_HEAD_DIM)
        if rpos is not None:
            k = _rope_2d(k, *rpos)
        return k, v

    def q_of(qp, rpos):
        bsz, m, _ = qp.shape
        q = _rms(qp.reshape(bsz, m, GQA_HEADS, GQA_HEAD_DIM), q_norm)
        if rpos is not None:
            q = _rope_2d(q, *rpos)
        return q.reshape(bsz, m, GQA_KV_HEADS, GQA_GROUP, GQA_HEAD_DIM)

    def branch(p, attn):
        return (attn * jax.nn.silu(p[..., o_v:])) @ w_out

    p = h @ w_in
    k_l, v_l = kv_of(p[..., o_q:o_v], pos)
    if need_ctx:
        pc = hc @ w_in
        kvc = pc[..., o_q:o_v]
    else:
        kvc = hc @ w_in[:, o_q:o_v]
    k_c, v_c = kv_of(kvc, None)
    k = jnp.concatenate([k_l, k_c], axis=1)
    v = jnp.concatenate([v_l, v_c], axis=1)
    y = branch(p, _attend(q_of(p[..., :o_q], pos), k, v))
    y_ctx = branch(pc, _attend(q_of(pc[..., :o_q], None), k_c, v_c)) if need_ctx else None
    return y, y_ctx


def setup_inputs(seed: int = 0) -> dict:
    key = jax.random.key(seed)
    ks = jax.random.split(key, 24)
    f32 = jnp.float32

    def nrm(k, shape, scale):
        return jax.random.normal(k, shape, f32) * scale

    def gain(k, shape):
        return 1.0 + 0.02 * jax.random.normal(k, shape, f32)

    return {
        'x': nrm(ks[0], (BATCH, SEQ, D_MODEL), 1.0),
        'c': nrm(ks[1], (BATCH, D_MODEL), 1.0),
        'ctx': nrm(ks[2], (BATCH, CTX_LEN, D_MODEL), 1.0),
        'c_ctx': nrm(ks[3], (D_MODEL,), 1.0),
        'w_mod': nrm(ks[4], (DEPTH, D_MODEL, 3 * D_MODEL), 0.5 * D_MODEL ** -0.5),
        'b_mod': nrm(ks[5], (DEPTH, 3 * D_MODEL), 0.01),
        'ln_g': gain(ks[6], (DEPTH, D_MODEL)),
        'ln_b': nrm(ks[7], (DEPTH, D_MODEL), 0.02),
        'a_w_in': nrm(ks[8], (N_EVEN, D_MODEL, EVEN_IN), D_MODEL ** -0.5),
        'a_conv_w': nrm(ks[9], (N_EVEN, CONV_SIZE, CONV_WIDTH), CONV_SIZE ** -0.5),
        'a_q_norm': gain(ks[10], (N_EVEN, MLA_Q_RANK)),
        'a_w_qb': nrm(ks[11], (N_EVEN, MLA_Q_RANK, MLA_HEADS * (MLA_NOPE + MLA_ROPE)), MLA_Q_RANK ** -0.5),
        'a_kv_norm': gain(ks[12], (N_EVEN, MLA_KV_RANK)),
        'a_w_kvb': nrm(ks[13], (N_EVEN, MLA_KV_RANK, MLA_HEADS * (MLA_NOPE + MLA_V)), MLA_KV_RANK ** -0.5),
        'a_w_out': nrm(ks[14], (N_EVEN, MIX_EVEN, D_MODEL), BETA * MIX_EVEN ** -0.5),
        'c_w_in': nrm(ks[15], (N_ODD, D_MODEL, ODD_IN), D_MODEL ** -0.5),
        'c_q_norm': gain(ks[16], (N_ODD, GQA_HEAD_DIM)),
        'c_k_norm': gain(ks[17], (N_ODD, GQA_HEAD_DIM)),
        'c_w_out': nrm(ks[18], (N_ODD, MIX_ODD, D_MODEL), BETA * MIX_ODD ** -0.5),
    }


def reference(x, c, ctx, c_ctx, w_mod, b_mod, ln_g, ln_b, a_w_in, a_conv_w, a_q_norm, a_w_qb,
              a_kv_norm, a_w_kvb, a_w_out, c_w_in, c_q_norm, c_k_norm, c_w_out):
    n = x.shape[1]
    rows = n // GRID_W
    pos_row = jnp.broadcast_to(jnp.arange(rows)[:, None], (rows, GRID_W)).reshape(-1).astype(jnp.float32)
    pos_col = jnp.broadcast_to(jnp.arange(GRID_W)[None, :], (rows, GRID_W)).reshape(-1).astype(jnp.float32)
    pos = (pos_row, pos_col)

    s_lat = jax.nn.silu(c)
    s_ctx = jax.nn.silu(c_ctx)
    h_ctx = ctx
    for layer in range(DEPTH):
        last = layer == DEPTH - 1
        shift, scale, gate = jnp.split(s_lat @ w_mod[layer] + b_mod[layer], 3, axis=-1)
        shift_c, scale_c, gate_c = jnp.split(s_ctx @ w_mod[layer] + b_mod[layer], 3, axis=-1)
        x_in = x * (1.0 + scale[:, None, :]) + shift[:, None, :]
        c_in = h_ctx * (1.0 + scale_c) + shift_c
        if layer % 2 == 0:
            i = layer // 2
            y, y_ctx = _even_mixer(x_in, c_in, a_w_in[i], a_conv_w[i], a_q_norm[i], a_w_qb[i],
                                   a_kv_norm[i], a_w_kvb[i], a_w_out[i], pos, not last)
        else:
            i = layer // 2
            y, y_ctx = _odd_mixer(x_in, c_in, c_w_in[i], c_q_norm[i], c_k_norm[i], c_w_out[i], pos, not last)
        x = _post_norm(x, gate[:, None, :] * y, ln_g[layer], ln_b[layer])
        if not last:
            h_ctx = _post_norm(h_ctx, gate_c * y_ctx, ln_g[layer], ln_b[layer])
    return x
```

```python
import functools
import math

import jax
import jax.numpy as jnp
import numpy as np
from jax import lax
from jax.experimental import pallas as pl
from jax.experimental.pallas import tpu as pltpu

F32 = jnp.float32
BF16 = jnp.bfloat16

D_MODEL = 2048
DEPTH = 2
GRID_W = 64
ROPE_THETA = 10000.0
EPS = 1e-6
ALPHA = (2 * DEPTH) ** 0.25

CONV_WIDTH = D_MODEL // 2
MLA_HEADS = D_MODEL // 256
MLA_Q_RANK = 512
MLA_KV_RANK = 512
MLA_NOPE = 128
MLA_ROPE = 64
MLA_V = 128
MLA_QK_PAD = 256
GQA_HEAD_DIM = 128
GQA_HEADS = D_MODEL // GQA_HEAD_DIM
GQA_KV_HEADS = GQA_HEADS // 4
GQA_GROUP = GQA_HEADS // GQA_KV_HEADS

LANES = 128
MOD_ROWS = 16
VMEM_LIMIT = 56 * 1024 * 1024
LOG2E = math.log2(math.e)


def _params(*sem):
    return pltpu.CompilerParams(dimension_semantics=sem, vmem_limit_bytes=VMEM_LIMIT)


def _silu(v):
    return v * jax.nn.sigmoid(v)


def _mod_kernel(c_ref, w_ref, b_ref, o_ref):
    s = _silu(c_ref[...]).astype(BF16)
    o_ref[0] = jnp.dot(s, w_ref[0].astype(BF16), preferred_element_type=F32) + b_ref[0]


def _mod_vectors(cvec, w_mod, b_mod):
    depth, d, n3 = w_mod.shape
    tn = 768
    return pl.pallas_call(
        _mod_kernel,
        grid=(depth, n3 // tn),
        in_specs=[
            pl.BlockSpec((MOD_ROWS, d), lambda l, j: (0, 0)),
            pl.BlockSpec((1, d, tn), lambda l, j: (l, 0, j)),
            pl.BlockSpec((1, 1, tn), lambda l, j: (l, 0, j)),
        ],
        out_specs=pl.BlockSpec((1, MOD_ROWS, tn), lambda l, j: (l, 0, j)),
        out_shape=jax.ShapeDtypeStruct((depth, MOD_ROWS, n3), F32),
        compiler_params=_params("arbitrary", "arbitrary"),
        name="mod_vectors",
    )(cvec, w_mod, b_mod.reshape(depth, 1, n3))


def _rms_rope_heads(acc, o_ref, g_ref, cos_ref, sin_ref, n_heads, col0):
    g = g_ref[...]
    cos = cos_ref[...]
    sin = sin_ref[...]
    for h in range(n_heads):
        lo = col0 + h * LANES
        y = acc[:, lo:lo + LANES]
        ms = jnp.mean(y * y, axis=-1, keepdims=True)
        yn = y * lax.rsqrt(ms + EPS) * g
        o_ref[:, lo:lo + LANES] = (yn * cos + pltpu.roll(yn, LANES // 2, 1) * sin).astype(o_ref.dtype)


def _mm_kernel(x_ref, sh_ref, sc_ref, w_ref, *rest, epilogue):
    *extra, o_ref, xin_ref = rest

    @pl.when(pl.program_id(1) == 0)
    def _():
        xin_ref[...] = (x_ref[...] * (1.0 + sc_ref[0]) + sh_ref[0]).astype(BF16)

    acc = jnp.dot(xin_ref[...], w_ref[...], preferred_element_type=F32)
    if epilogue == "none":
        o_ref[...] = acc.astype(o_ref.dtype)
    elif epilogue == "silu":
        o_ref[...] = _silu(acc).astype(o_ref.dtype)
    elif epilogue == "q_rope":
        g_ref, cos_ref, sin_ref = extra
        _rms_rope_heads(acc, o_ref, g_ref, cos_ref, sin_ref, acc.shape[1] // LANES, 0)
    elif epilogue == "kv_rope":
        g_ref, cos_ref, sin_ref = extra
        half = acc.shape[1] // 2
        _rms_rope_heads(acc, o_ref, g_ref, cos_ref, sin_ref, half // LANES, 0)
        o_ref[:, half:] = acc[:, half:].astype(o_ref.dtype)
    else:
        raise ValueError(epilogue)


def _mod_matmul(x, mod3, w, *, rows_per_mod, mod_row0, tm, tn, epilogue="none", extra=(), pos_rows=None):
    m, d = x.shape
    n = w.shape[1]
    assert m % tm == 0 and n % tn == 0

    if rows_per_mod is None:
        mod_idx = lambda i: mod_row0
    else:
        assert rows_per_mod % tm == 0
        tiles_per_mod = rows_per_mod // tm
        mod_idx = lambda i: mod_row0 + i // tiles_per_mod

    in_specs = [
        pl.BlockSpec((tm, d), lambda i, j: (i, 0)),
        pl.BlockSpec((1, 1, d), lambda i, j: (mod_idx(i), 0, 0)),
        pl.BlockSpec((1, 1, d), lambda i, j: (mod_idx(i), 0, 1)),
        pl.BlockSpec((d, tn), lambda i, j: (0, j)),
    ]
    if extra:
        assert pos_rows % tm == 0
        tiles_per_seq = pos_rows // tm
        in_specs += [
            pl.BlockSpec((1, LANES), lambda i, j: (0, 0)),
            pl.BlockSpec((tm, LANES), lambda i, j: (i % tiles_per_seq, 0)),
            pl.BlockSpec((tm, LANES), lambda i, j: (i % tiles_per_seq, 0)),
        ]
    return pl.pallas_call(
        functools.partial(_mm_kernel, epilogue=epilogue),
        grid=(m // tm, n // tn),
        in_specs=in_specs,
        out_specs=pl.BlockSpec((tm, tn), lambda i, j: (i, j)),
        out_shape=jax.ShapeDtypeStruct((m, n), BF16),
        scratch_shapes=[pltpu.VMEM((tm, d), BF16)],
        compiler_params=_params("arbitrary", "arbitrary"),
        name="mod_matmul_" + epilogue,
    )(x, mod3, mod3, w, *extra)


def _rope_pair_sum(t):
    return t + pltpu.roll(t, LANES // 2, 1)


def _mla_proj_kernel(ql_ref, kvl_ref, kr_ref, qg_ref, kvg_ref, wq_ref, wkv_ref, cs_ref,
                     q_ref, k_ref, v_ref):
    cs = cs_ref[...]

    def rms(v, g):
        vf = v.astype(F32)
        return (vf * lax.rsqrt(jnp.mean(vf * vf, axis=-1, keepdims=True) + EPS) * g).astype(BF16)

    yq = jnp.dot(rms(ql_ref[...], qg_ref[...]), wq_ref[...], preferred_element_type=F32)
    for h in range(MLA_HEADS):
        lo = h * MLA_QK_PAD
        q_ref[:, lo:lo + MLA_NOPE] = yq[:, lo:lo + MLA_NOPE].astype(q_ref.dtype)
        q_ref[:, lo + MLA_NOPE:lo + MLA_QK_PAD] = _rope_pair_sum(
            yq[:, lo + MLA_NOPE:lo + MLA_QK_PAD] * cs).astype(q_ref.dtype)

    ykv = jnp.dot(rms(kvl_ref[...], kvg_ref[...]), wkv_ref[...], preferred_element_type=F32)
    kr = _rope_pair_sum(kr_ref[...].astype(F32) * cs)
    lane = lax.broadcasted_iota(jnp.int32, kr.shape, 1)
    kr = jnp.where(lane < MLA_ROPE, kr, 0.0).astype(k_ref.dtype)
    for h in range(MLA_HEADS):
        lo = h * MLA_QK_PAD
        k_ref[:, lo:lo + MLA_NOPE] = ykv[:, h * MLA_NOPE:(h + 1) * MLA_NOPE].astype(k_ref.dtype)
        k_ref[:, lo + MLA_NOPE:lo + MLA_QK_PAD] = kr
    v_ref[...] = ykv[:, MLA_HEADS * MLA_NOPE:].astype(v_ref.dtype)


def _mla_proj(p_lat, q_gain, kv_gain, wq, wkv, cs_tab, *, tm):
    m = p_lat.shape[0]
    pos_rows = cs_tab.shape[0]
    assert m % tm == 0 and pos_rows % tm == 0
    tiles_per_seq = pos_rows // tm
    hq = MLA_HEADS * MLA_QK_PAD
    hv = MLA_HEADS * MLA_V
    rope_blk = (MLA_Q_RANK + MLA_KV_RANK) // LANES
    return pl.pallas_call(
        _mla_proj_kernel,
        grid=(m // tm,),
        in_specs=[
            pl.BlockSpec((tm, MLA_Q_RANK), lambda i: (i, 0)),
            pl.BlockSpec((tm, MLA_KV_RANK), lambda i: (i, 1)),
            pl.BlockSpec((tm, LANES), lambda i: (i, rope_blk)),
            pl.BlockSpec((1, MLA_Q_RANK), lambda i: (0, 0)),
            pl.BlockSpec((1, MLA_KV_RANK), lambda i: (0, 0)),
            pl.BlockSpec((MLA_Q_RANK, hq), lambda i: (0, 0)),
            pl.BlockSpec((MLA_KV_RANK, hv + hv), lambda i: (0, 0)),
            pl.BlockSpec((tm, LANES), lambda i: (i % tiles_per_seq, 0)),
        ],
        out_specs=[
            pl.BlockSpec((tm, hq), lambda i: (i, 0)),
            pl.BlockSpec((tm, hq), lambda i: (i, 0)),
            pl.BlockSpec((tm, hv), lambda i: (i, 0)),
        ],
        out_shape=[
            jax.ShapeDtypeStruct((m, hq), BF16),
            jax.ShapeDtypeStruct((m, hq), BF16),
            jax.ShapeDtypeStruct((m, hv), BF16),
        ],
        compiler_params=_params("arbitrary"),
        name="mla_proj",
    )(p_lat, p_lat, p_lat, q_gain, kv_gain, wq, wkv, cs_tab)


def _attn_kernel(*refs, exp2_scale, sub, bkv, lat_keys):
    if lat_keys:
        q_ref, kl_ref, vl_ref, kc_ref, vc_ref, o_ref = refs
    else:
        q_ref, kc_ref, vc_ref, o_ref = refs
    dv = o_ref.shape[-1]

    def step(q, k, v, carry):
        m, l, acc = carry
        s = lax.dot_general(q, k, (((1,), (1,)), ((), ())), preferred_element_type=F32)
        m_new = jnp.maximum(m, jnp.max(s, axis=-1, keepdims=True))
        p = jnp.exp2((s - m_new) * exp2_scale)
        alpha = jnp.exp2((m - m_new) * exp2_scale)
        l = alpha * l + jnp.sum(p, axis=-1, keepdims=True)
        acc = alpha * acc + jnp.dot(p.astype(v.dtype), v, preferred_element_type=F32)
        return m_new, l, acc

    def one_sub(si, _):
        r0 = pl.multiple_of(si * sub, sub)
        q = q_ref[pl.ds(r0, sub), :]
        carry = (jnp.full((sub, 1), -jnp.inf, F32), jnp.zeros((sub, 1), F32), jnp.zeros((sub, dv), F32))
        if lat_keys:
            def body(j, carry):
                c0 = pl.multiple_of(j * bkv, bkv)
                return step(q, kl_ref[pl.ds(c0, bkv), :], vl_ref[pl.ds(c0, bkv), :], carry)
            carry = lax.fori_loop(0, lat_keys // bkv, body, carry)
        _, l, acc = step(q, kc_ref[...], vc_ref[...], carry)
        o_ref[pl.ds(r0, sub), :] = (acc / l).astype(o_ref.dtype)
        return 0

    lax.fori_loop(0, q_ref.shape[0] // sub, one_sub, 0)


def _attention(q, kv_lat, kv_ctx, *, batch, heads, group, dk, dv, k_col0, v_col0, scale, bq,
               lat_keys, ctx_keys):
    mq = q.shape[0]
    nq = mq // batch
    assert nq % bq == 0
    q_tiles = nq // bq
    sub = min(bq, 256)
    bkv = 512
    q_spec = pl.BlockSpec((bq, dk), lambda b, h, i: (b * q_tiles + i, h))
    in_specs = [q_spec]
    operands = [q]
    if kv_lat is not None:
        assert lat_keys % bkv == 0
        in_specs += [
            pl.BlockSpec((lat_keys, dk), lambda b, h, i: (b, k_col0 + h // group)),
            pl.BlockSpec((lat_keys, dv), lambda b, h, i: (b, v_col0 + h // group)),
        ]
        operands += list(kv_lat)
    in_specs += [
        pl.BlockSpec((ctx_keys, dk), lambda b, h, i: (b, k_col0 + h // group)),
        pl.BlockSpec((ctx_keys, dv), lambda b, h, i: (b, v_col0 + h // group)),
    ]
    operands += list(kv_ctx)
    return pl.pallas_call(
        functools.partial(_attn_kernel, exp2_scale=scale * LOG2E, sub=sub, bkv=bkv,
                          lat_keys=lat_keys if kv_lat is not None else 0),
        grid=(batch, heads, q_tiles),
        in_specs=in_specs,
        out_specs=pl.BlockSpec((bq, dv), lambda b, h, i: (b * q_tiles + i, h)),
        out_shape=jax.ShapeDtypeStruct((mq, heads * dv), BF16),
        compiler_params=_params("arbitrary", "arbitrary", "arbitrary"),
        name="attention",
    )(*operands)


def _conv_kernel(cb_ref, cc_ref, cu_ref, w_ref, o_ref):
    t = cc_ref[...].astype(F32) * cu_ref[...].astype(F32)
    n = t.shape[0]
    row = lax.broadcasted_iota(jnp.int32, t.shape, 0)
    t_prev = jnp.where(row == 0, 0.0, pltpu.roll(t, 1, 0))
    t_next = jnp.where(row == n - 1, 0.0, pltpu.roll(t, n - 1, 0))
    w = w_ref[...]
    conv = t_prev * w[0:1, :] + t * w[1:2, :] + t_next * w[2:3, :]
    o_ref[...] = (cb_ref[...].astype(F32) * conv).astype(o_ref.dtype)


def _short_conv_gate(p_conv, conv_w, *, batch):
    m = p_conv.shape[0]
    n = m // batch
    cblk = CONV_WIDTH // LANES
    return pl.pallas_call(
        _conv_kernel,
        grid=(batch, cblk),
        in_specs=[
            pl.BlockSpec((n, LANES), lambda b, j: (b, j)),
            pl.BlockSpec((n, LANES), lambda b, j: (b, cblk + j)),
            pl.BlockSpec((n, LANES), lambda b, j: (b, 2 * cblk + j)),
            pl.BlockSpec((3, LANES), lambda b, j: (0, j)),
        ],
        out_specs=pl.BlockSpec((n, LANES), lambda b, j: (b, j)),
        out_shape=jax.ShapeDtypeStruct((m, CONV_WIDTH), BF16),
        compiler_params=_params("arbitrary", "arbitrary"),
        name="short_conv",
    )(p_conv, p_conv, p_conv, conv_w)


def _out_kernel(*refs, n_parts):
    parts = refs[:n_parts]
    g_ref, w_ref, x_ref, gate_ref, lng_ref, lnb_ref, o_ref = refs[n_parts:]
    y = None
    off = 0
    for p_ref in parts:
        kp = p_ref.shape[1]
        u = p_ref[...] * g_ref[:, off:off + kp]
        part = jnp.dot(u, w_ref[off:off + kp, :], preferred_element_type=F32)
        y = part if y is None else y + part
        off += kp
    z = ALPHA * x_ref[...] + gate_ref[0] * y
    mu = jnp.mean(z, axis=-1, keepdims=True)
    zc = z - mu
    var = jnp.mean(zc * zc, axis=-1, keepdims=True)
    o_ref[...] = zc * lax.rsqrt(var + EPS) * lng_ref[...] + lnb_ref[...]


def _out_proj_norm(parts, g, w, x, mod3, ln_g, ln_b, *, rows_per_mod, mod_row0, tm):
    m, d = x.shape
    assert m % tm == 0
    if rows_per_mod is None:
        mod_idx = lambda i: mod_row0
    else:
        tiles_per_mod = rows_per_mod // tm
        mod_idx = lambda i: mod_row0 + i // tiles_per_mod
    in_specs = [pl.BlockSpec((tm, p.shape[1]), lambda i: (i, 0)) for p in parts]
    in_specs += [
        pl.BlockSpec((tm, g.shape[1]), lambda i: (i, 0)),
        pl.BlockSpec(w.shape, lambda i: (0, 0)),
        pl.BlockSpec((tm, d), lambda i: (i, 0)),
        pl.BlockSpec((1, 1, d), lambda i: (mod_idx(i), 0, 2)),
        pl.BlockSpec((1, d), lambda i: (0, 0)),
        pl.BlockSpec((1, d), lambda i: (0, 0)),
    ]
    return pl.pallas_call(
        functools.partial(_out_kernel, n_parts=len(parts)),
        grid=(m // tm,),
        in_specs=in_specs,
        out_specs=pl.BlockSpec((tm, d), lambda i: (i, 0)),
        out_shape=jax.ShapeDtypeStruct((m, d), F32),
        compiler_params=_params("arbitrary"),
        name="out_proj_norm",
    )(*parts, g, w, x, mod3, ln_g, ln_b)


def _rope_tables(n, ctx_len):
    rows = n // GRID_W
    pos_row = np.repeat(np.arange(rows, dtype=np.float32), GRID_W)
    pos_col = np.tile(np.arange(GRID_W, dtype=np.float32), rows)

    def angles(pos, half):
        inv = jnp.asarray(ROPE_THETA, F32) ** (-jnp.arange(half, dtype=F32) / half)
        return jnp.asarray(pos)[:, None] * inv[None, :]

    ar, ac = angles(pos_row, MLA_ROPE // 4), angles(pos_col, MLA_ROPE // 4)
    cs_mla = jnp.concatenate([jnp.cos(ar), jnp.cos(ar), jnp.cos(ac), jnp.cos(ac),
                              -jnp.sin(ar), jnp.sin(ar), -jnp.sin(ac), jnp.sin(ac)], axis=1)
    cs_mla_ctx = jnp.concatenate([jnp.ones((ctx_len, MLA_ROPE), F32), jnp.zeros((ctx_len, MLA_ROPE), F32)], axis=1)
    br, bc = angles(pos_row, GQA_HEAD_DIM // 4), angles(pos_col, GQA_HEAD_DIM // 4)
    cos_gqa = jnp.concatenate([jnp.cos(br), jnp.cos(bc), jnp.cos(br), jnp.cos(bc)], axis=1)
    sin_gqa = jnp.concatenate([-jnp.sin(br), -jnp.sin(bc), jnp.sin(br), jnp.sin(bc)], axis=1)
    cos_ctx = jnp.ones((ctx_len, GQA_HEAD_DIM), F32)
    sin_ctx = jnp.zeros((ctx_len, GQA_HEAD_DIM), F32)
    return cs_mla, cs_mla_ctx, cos_gqa, sin_gqa, cos_ctx, sin_ctx


def kernel(x, c, ctx, c_ctx, w_mod, b_mod, ln_g, ln_b, a_w_in, a_conv_w, a_q_norm, a_w_qb, a_kv_norm,
           a_w_kvb, a_w_out, c_w_in, c_q_norm, c_k_norm, c_w_out):
    batch, n, d = x.shape
    ctx_len = ctx.shape[1]
    assert d == D_MODEL and batch + 1 <= MOD_ROWS
    ctx_mod_row = batch

    cs_mla, cs_mla_ctx, cos_gqa, sin_gqa, cos_ctx, sin_ctx = _rope_tables(n, ctx_len)

    cvec = jnp.concatenate([c, c_ctx[None, :], jnp.zeros((MOD_ROWS - batch - 1, d), F32)], axis=0)
    mods = _mod_vectors(cvec, w_mod, b_mod)
    mod0 = mods[0].reshape(MOD_ROWS, 1, 3 * d)
    mod1 = mods[1].reshape(MOD_ROWS, 1, 3 * d)

    xl = x.reshape(batch * n, d)
    xc = ctx.reshape(batch * ctx_len, d)

    w_in0 = a_w_in[0]
    o_c = 3 * CONV_WIDTH
    o_q = o_c + MLA_Q_RANK
    o_kv = o_q + MLA_KV_RANK
    o_g = o_kv + MLA_ROPE
    swap16 = np.concatenate([np.arange(16, 32), np.arange(0, 16), np.arange(48, 64), np.arange(32, 48)])
    w_conv = w_in0[:, :o_c].astype(BF16)
    w_rope = w_in0[:, o_kv:o_g]
    w_lat = jnp.concatenate([w_in0[:, o_c:o_kv], w_rope, w_rope[:, swap16]], axis=1).astype(BF16)
    w_gate0 = w_in0[:, o_g:].astype(BF16)
    wq3 = a_w_qb[0].reshape(MLA_Q_RANK, MLA_HEADS, MLA_NOPE + MLA_ROPE)
    wq_rope = wq3[:, :, MLA_NOPE:]
    wq = jnp.concatenate([wq3[:, :, :MLA_NOPE], wq_rope, wq_rope[:, :, swap16]], axis=2)
    wq = wq.reshape(MLA_Q_RANK, MLA_HEADS * MLA_QK_PAD).astype(BF16)
    wkv3 = a_w_kvb[0].reshape(MLA_KV_RANK, MLA_HEADS, MLA_NOPE + MLA_V)
    wkv = jnp.concatenate([wkv3[:, :, :MLA_NOPE].reshape(MLA_KV_RANK, -1),
                           wkv3[:, :, MLA_NOPE:].reshape(MLA_KV_RANK, -1)], axis=1).astype(BF16)
    w_out0 = a_w_out[0].astype(BF16)

    w_in1 = c_w_in[0]
    nq1 = GQA_HEADS * GQA_HEAD_DIM
    nk1 = GQA_KV_HEADS * GQA_HEAD_DIM
    perm = np.concatenate([np.arange(0, 32), np.arange(64, 96), np.arange(32, 64), np.arange(96, 128)])
    w_q1 = w_in1[:, :nq1].reshape(d, GQA_HEADS, GQA_HEAD_DIM)[:, :, perm].reshape(d, nq1).astype(BF16)
    w_k1 = w_in1[:, nq1:nq1 + nk1].reshape(d, GQA_KV_HEADS, GQA_HEAD_DIM)[:, :, perm].reshape(d, nk1)
    w_kv1 = jnp.concatenate([w_k1, w_in1[:, nq1 + nk1:nq1 + 2 * nk1]], axis=1).astype(BF16)
    w_gate1 = w_in1[:, nq1 + 2 * nk1:].astype(BF16)
    q_gain1 = c_q_norm[0][perm].reshape(1, GQA_HEAD_DIM)
    k_gain1 = c_k_norm[0][perm].reshape(1, GQA_HEAD_DIM)
    w_out1 = c_w_out[0].astype(BF16)

    lng0, lnb0 = ln_g[0].reshape(1, d), ln_b[0].reshape(1, d)
    lng1, lnb1 = ln_g[1].reshape(1, d), ln_b[1].reshape(1, d)

    lat = dict(rows_per_mod=n, mod_row0=0)
    cxt = dict(rows_per_mod=None, mod_row0=ctx_mod_row)
    tm_l, tm_c = 1024, ctx_len

    def layer0_front(rows, mod_kw, tm_proj, cs_tab):
        p_conv = _mod_matmul(rows, mod0, w_conv, tm=tm_l, tn=1024, **mod_kw)
        p_lat = _mod_matmul(rows, mod0, w_lat, tm=tm_l, tn=w_lat.shape[1], **mod_kw)
        gate = _mod_matmul(rows, mod0, w_gate0, tm=tm_l, tn=1024, epilogue="silu", **mod_kw)
        q, k, v = _mla_proj(p_lat, a_q_norm[0].reshape(1, -1), a_kv_norm[0].reshape(1, -1), wq, wkv,
                            cs_tab, tm=tm_proj)
        return p_conv, gate, q, k, v

    pcl, gl, ql, kl, vl = layer0_front(xl, lat, 512, cs_mla)
    pcc, gc, qc, kc, vc = layer0_front(xc, cxt, tm_c, cs_mla_ctx)

    mla = dict(batch=batch, heads=MLA_HEADS, group=1, dk=MLA_QK_PAD, dv=MLA_V, k_col0=0, v_col0=0,
               scale=(MLA_NOPE + MLA_ROPE) ** -0.5, lat_keys=n, ctx_keys=ctx_len)
    attn_l = _attention(ql, (kl, vl), (kc, vc), bq=1024, **mla)
    attn_c = _attention(qc, None, (kc, vc), bq=ctx_len, **mla)

    conv_l = _short_conv_gate(pcl, a_conv_w[0], batch=batch)
    conv_c = _short_conv_gate(pcc, a_conv_w[0], batch=batch)

    x1 = _out_proj_norm([conv_l, attn_l], gl, w_out0, xl, mod0, lng0, lnb0, tm=256, **lat)
    h1 = _out_proj_norm([conv_c, attn_c], gc, w_out0, xc, mod0, lng0, lnb0, tm=256, **cxt)

    q1 = _mod_matmul(x1, mod1, w_q1, tm=tm_l, tn=1024, epilogue="q_rope",
                     extra=(q_gain1, cos_gqa, sin_gqa), pos_rows=n, **lat)
    kv1 = _mod_matmul(x1, mod1, w_kv1, tm=tm_l, tn=2 * nk1, epilogue="kv_rope",
                      extra=(k_gain1, cos_gqa, sin_gqa), pos_rows=n, **lat)
    g1 = _mod_matmul(x1, mod1, w_gate1, tm=tm_l, tn=1024, epilogue="silu", **lat)
    kvc1 = _mod_matmul(h1, mod1, w_kv1, tm=tm_c, tn=2 * nk1, epilogue="kv_rope",
                       extra=(k_gain1, cos_ctx, sin_ctx), pos_rows=ctx_len, **cxt)

    attn1 = _attention(q1, (kv1, kv1), (kvc1, kvc1), batch=batch, heads=GQA_HEADS, group=GQA_GROUP,
                       dk=GQA_HEAD_DIM, dv=GQA_HEAD_DIM, k_col0=0, v_col0=GQA_KV_HEADS,
                       scale=GQA_HEAD_DIM ** -0.5, bq=1024, lat_keys=n, ctx_keys=ctx_len)

    out = _out_proj_norm([attn1], g1, w_out1, x1, mod1, lng1, lnb1, tm=256, **lat)
    return out.reshape(batch, n, d)
```

```python
import functools
import math

import jax
import jax.numpy as jnp
import numpy as np
from jax import lax
from jax.experimental import pallas as pl
from jax.experimental.pallas import tpu as pltpu

F32 = jnp.float32
BF16 = jnp.bfloat16

D_MODEL = 2048
DEPTH = 2
GRID_W = 64
ROPE_THETA = 10000.0
EPS = 1e-6
ALPHA = (2 * DEPTH) ** 0.25

CONV_WIDTH = D_MODEL // 2
MLA_HEADS = D_MODEL // 256
MLA_Q_RANK = 512
MLA_KV_RANK = 512
MLA_NOPE = 128
MLA_ROPE = 64
MLA_V = 128
MLA_QK_PAD = 256
GQA_HEAD_DIM = 128
GQA_HEADS = D_MODEL // GQA_HEAD_DIM
GQA_KV_HEADS = GQA_HEADS // 4
GQA_GROUP = GQA_HEADS // GQA_KV_HEADS

LANES = 128
BF16_SUBLANES = 16
HEAD_V = 128
HEAD_VX = HEAD_V + BF16_SUBLANES
MOD_ROWS = 16
VMEM_LIMIT = 56 * 1024 * 1024
LOG2E = math.log2(math.e)


def _params(*sem):
    return pltpu.CompilerParams(dimension_semantics=sem, vmem_limit_bytes=VMEM_LIMIT)


def _silu(v):
    return v * jax.nn.sigmoid(v)


def _store_values_t(vt_ref, v, n_heads):
    rows = v.shape[0]
    for h in range(n_heads):
        lo = h * HEAD_VX
        vt_ref[lo:lo + HEAD_V, :] = v[:, h * HEAD_V:(h + 1) * HEAD_V].T.astype(vt_ref.dtype)
        vt_ref[lo + HEAD_V:lo + HEAD_VX, :] = jnp.ones((BF16_SUBLANES, rows), vt_ref.dtype)


def _mod_kernel(c_ref, w_ref, b_ref, o_ref):
    s = _silu(c_ref[...]).astype(BF16)
    o_ref[0] = jnp.dot(s, w_ref[0].astype(BF16), preferred_element_type=F32) + b_ref[0]


def _mod_vectors(cvec, w_mod, b_mod):
    depth, d, n3 = w_mod.shape
    tn = 768
    return pl.pallas_call(
        _mod_kernel,
        grid=(depth, n3 // tn),
        in_specs=[
            pl.BlockSpec((MOD_ROWS, d), lambda l, j: (0, 0)),
            pl.BlockSpec((1, d, tn), lambda l, j: (l, 0, j)),
            pl.BlockSpec((1, 1, tn), lambda l, j: (l, 0, j)),
        ],
        out_specs=pl.BlockSpec((1, MOD_ROWS, tn), lambda l, j: (l, 0, j)),
        out_shape=jax.ShapeDtypeStruct((depth, MOD_ROWS, n3), F32),
        compiler_params=_params("arbitrary", "arbitrary"),
        name="mod_vectors",
    )(cvec, w_mod, b_mod.reshape(depth, 1, n3))


def _rms_rope(y, g, cos, sin):
    ms = jnp.mean(y * y, axis=-1, keepdims=True)
    yn = y * lax.rsqrt(ms + EPS) * g
    return yn * cos + pltpu.roll(yn, LANES // 2, 1) * sin


def _mm_kernel(x_ref, sh_ref, sc_ref, w_ref, *rest, epilogue, n_out, q_scale):
    extra = rest[:len(rest) - n_out - 1]
    outs = rest[len(rest) - n_out - 1:-1]
    xin_ref = rest[-1]

    @pl.when(pl.program_id(1) == 0)
    def _():
        xin_ref[...] = (x_ref[...] * (1.0 + sc_ref[0]) + sh_ref[0]).astype(BF16)

    acc = jnp.dot(xin_ref[...], w_ref[...], preferred_element_type=F32)
    if epilogue == "none":
        outs[0][...] = acc.astype(outs[0].dtype)
    elif epilogue == "silu":
        outs[0][...] = _silu(acc).astype(outs[0].dtype)
    elif epilogue == "q_rope":
        g_ref, cos_ref, sin_ref = extra
        g, cos, sin = g_ref[...], cos_ref[...], sin_ref[...]
        for h in range(acc.shape[1] // LANES):
            r = _rms_rope(acc[:, h * LANES:(h + 1) * LANES], g, cos, sin) * q_scale
            outs[0][h * LANES:(h + 1) * LANES, :] = r.T.astype(outs[0].dtype)
    elif epilogue == "kv_rope":
        g_ref, cos_ref, sin_ref = extra
        g, cos, sin = g_ref[...], cos_ref[...], sin_ref[...]
        k_ref, vt_ref = outs
        half = acc.shape[1] // 2
        for h in range(half // LANES):
            k_ref[:, h * LANES:(h + 1) * LANES] = _rms_rope(
                acc[:, h * LANES:(h + 1) * LANES], g, cos, sin).astype(k_ref.dtype)
        _store_values_t(vt_ref, acc[:, half:], half // HEAD_V)
    else:
        raise ValueError(epilogue)


def _mod_matmul(x, mod3, w, *, rows_per_mod, mod_row0, tm, tn, epilogue="none", extra=(), pos_rows=None,
                q_scale=None):
    m, d = x.shape
    n = w.shape[1]
    assert m % tm == 0 and n % tn == 0

    if rows_per_mod is None:
        mod_idx = lambda i: mod_row0
    else:
        assert rows_per_mod % tm == 0
        tiles_per_mod = rows_per_mod // tm
        mod_idx = lambda i: mod_row0 + i // tiles_per_mod

    in_specs = [
        pl.BlockSpec((tm, d), lambda i, j: (i, 0)),
        pl.BlockSpec((1, 1, d), lambda i, j: (mod_idx(i), 0, 0)),
        pl.BlockSpec((1, 1, d), lambda i, j: (mod_idx(i), 0, 1)),
        pl.BlockSpec((d, tn), lambda i, j: (0, j)),
    ]
    if extra:
        assert pos_rows % tm == 0
        tiles_per_seq = pos_rows // tm
        in_specs += [
            pl.BlockSpec((1, LANES), lambda i, j: (0, 0)),
            pl.BlockSpec((tm, LANES), lambda i, j: (i % tiles_per_seq, 0)),
            pl.BlockSpec((tm, LANES), lambda i, j: (i % tiles_per_seq, 0)),
        ]
    if epilogue == "q_rope":
        out_specs = pl.BlockSpec((tn, tm), lambda i, j: (j, i))
        out_shape = jax.ShapeDtypeStruct((n, m), BF16)
        n_out = 1
    elif epilogue == "kv_rope":
        assert tn == n
        vrows = (n // 2 // HEAD_V) * HEAD_VX
        out_specs = [pl.BlockSpec((tm, n // 2), lambda i, j: (i, 0)),
                     pl.BlockSpec((vrows, tm), lambda i, j: (0, i))]
        out_shape = [jax.ShapeDtypeStruct((m, n // 2), BF16), jax.ShapeDtypeStruct((vrows, m), BF16)]
        n_out = 2
    else:
        out_specs = pl.BlockSpec((tm, tn), lambda i, j: (i, j))
        out_shape = jax.ShapeDtypeStruct((m, n), BF16)
        n_out = 1
    return pl.pallas_call(
        functools.partial(_mm_kernel, epilogue=epilogue, n_out=n_out, q_scale=q_scale),
        grid=(m // tm, n // tn),
        in_specs=in_specs,
        out_specs=out_specs,
        out_shape=out_shape,
        scratch_shapes=[pltpu.VMEM((tm, d), BF16)],
        compiler_params=_params("arbitrary", "arbitrary"),
        name="mod_matmul_" + epilogue,
    )(x, mod3, mod3, w, *extra)


def _rope_pair_sum(t):
    return t + pltpu.roll(t, LANES // 2, 1)


def _mla_proj_kernel(ql_ref, kvl_ref, kr_ref, qg_ref, kvg_ref, wq_ref, wkv_ref, cs_ref,
                     qt_ref, k_ref, vt_ref, *, q_scale):
    cs = cs_ref[...]

    def rms(v, g):
        vf = v.astype(F32)
        return (vf * lax.rsqrt(jnp.mean(vf * vf, axis=-1, keepdims=True) + EPS) * g).astype(BF16)

    yq = jnp.dot(rms(ql_ref[...], qg_ref[...]), wq_ref[...], preferred_element_type=F32)
    for h in range(MLA_HEADS):
        lo = h * MLA_QK_PAD
        nope = yq[:, lo:lo + MLA_NOPE] * q_scale
        rope = _rope_pair_sum(yq[:, lo + MLA_NOPE:lo + MLA_QK_PAD] * cs) * q_scale
        qt_ref[lo:lo + MLA_NOPE, :] = nope.T.astype(qt_ref.dtype)
        qt_ref[lo + MLA_NOPE:lo + MLA_QK_PAD, :] = rope.T.astype(qt_ref.dtype)

    ykv = jnp.dot(rms(kvl_ref[...], kvg_ref[...]), wkv_ref[...], preferred_element_type=F32)
    kr = _rope_pair_sum(kr_ref[...].astype(F32) * cs)
    lane = lax.broadcasted_iota(jnp.int32, kr.shape, 1)
    kr = jnp.where(lane < MLA_ROPE, kr, 0.0).astype(k_ref.dtype)
    for h in range(MLA_HEADS):
        lo = h * MLA_QK_PAD
        k_ref[:, lo:lo + MLA_NOPE] = ykv[:, h * MLA_NOPE:(h + 1) * MLA_NOPE].astype(k_ref.dtype)
        k_ref[:, lo + MLA_NOPE:lo + MLA_QK_PAD] = kr
    _store_values_t(vt_ref, ykv[:, MLA_HEADS * MLA_NOPE:], MLA_HEADS)


def _mla_proj(p_lat, q_gain, kv_gain, wq, wkv, cs_tab, *, tm, q_scale):
    m = p_lat.shape[0]
    pos_rows = cs_tab.shape[0]
    assert m % tm == 0 and pos_rows % tm == 0
    tiles_per_seq = pos_rows // tm
    hq = MLA_HEADS * MLA_QK_PAD
    hv = MLA_HEADS * MLA_V
    hvx = MLA_HEADS * HEAD_VX
    rope_blk = (MLA_Q_RANK + MLA_KV_RANK) // LANES
    return pl.pallas_call(
        functools.partial(_mla_proj_kernel, q_scale=q_scale),
        grid=(m // tm,),
        in_specs=[
            pl.BlockSpec((tm, MLA_Q_RANK), lambda i: (i, 0)),
            pl.BlockSpec((tm, MLA_KV_RANK), lambda i: (i, 1)),
            pl.BlockSpec((tm, LANES), lambda i: (i, rope_blk)),
            pl.BlockSpec((1, MLA_Q_RANK), lambda i: (0, 0)),
            pl.BlockSpec((1, MLA_KV_RANK), lambda i: (0, 0)),
            pl.BlockSpec((MLA_Q_RANK, hq), lambda i: (0, 0)),
            pl.BlockSpec((MLA_KV_RANK, hv + hv), lambda i: (0, 0)),
            pl.BlockSpec((tm, LANES), lambda i: (i % tiles_per_seq, 0)),
        ],
        out_specs=[
            pl.BlockSpec((hq, tm), lambda i: (0, i)),
            pl.BlockSpec((tm, hq), lambda i: (i, 0)),
            pl.BlockSpec((hvx, tm), lambda i: (0, i)),
        ],
        out_shape=[
            jax.ShapeDtypeStruct((hq, m), BF16),
            jax.ShapeDtypeStruct((m, hq), BF16),
            jax.ShapeDtypeStruct((hvx, m), BF16),
        ],
        compiler_params=_params("arbitrary"),
        name="mla_proj",
    )(p_lat, p_lat, p_lat, q_gain, kv_gain, wq, wkv, cs_tab)


def _attn_kernel(*refs, sub, ck, lat_keys, ctx_keys):
    if lat_keys:
        qt_ref, kl_ref, vtl_ref, kc_ref, vtc_ref, o_ref, s_scr = refs
    else:
        qt_ref, kc_ref, vtc_ref, o_ref, s_scr = refs
        kl_ref = vtl_ref = None
    chunks = [(kl_ref, vtl_ref, j * ck, ck) for j in range(lat_keys // ck)]
    chunks.append((kc_ref, vtc_ref, 0, ctx_keys))

    for si in range(qt_ref.shape[1] // sub):
        qt = qt_ref[:, si * sub:(si + 1) * sub]
        mrun = None
        row = 0
        for k_ref, _, r0, rows in chunks:
            s = jnp.dot(k_ref[r0:r0 + rows, :], qt, preferred_element_type=F32)
            s_scr[si, row:row + rows, :] = s
            cm = jnp.max(s.reshape(rows // 8, 8, sub), axis=0)
            mrun = cm if mrun is None else jnp.maximum(mrun, cm)
            row += rows
        m = jnp.max(mrun, axis=0, keepdims=True)
        acc = None
        row = 0
        for _, vt_ref, r0, rows in chunks:
            p = jnp.exp2(s_scr[si, row:row + rows, :] - m).astype(BF16)
            part = jnp.dot(vt_ref[:, r0:r0 + rows], p, preferred_element_type=F32)
            acc = part if acc is None else acc + part
            row += rows
        ot = acc[:HEAD_V] / acc[HEAD_V:HEAD_V + 1]
        o_ref[si * sub:(si + 1) * sub, :] = ot.T.astype(o_ref.dtype)


def _attention(qt, kv_lat, kv_ctx, *, batch, heads, group, dk, bq, sub, lat_keys, ctx_keys):
    mq = qt.shape[1]
    nq = mq // batch
    assert nq % bq == 0 and bq % sub == 0
    q_tiles = nq // bq
    ck = 512
    in_specs = [pl.BlockSpec((dk, bq), lambda b, h, i: (h, b * q_tiles + i))]
    operands = [qt]
    n_keys = ctx_keys
    if kv_lat is not None:
        assert lat_keys % ck == 0
        n_keys += lat_keys
        in_specs += [
            pl.BlockSpec((lat_keys, dk), lambda b, h, i: (b, h // group)),
            pl.BlockSpec((HEAD_VX, lat_keys), lambda b, h, i: (h // group, b)),
        ]
        operands += list(kv_lat)
    in_specs += [
        pl.BlockSpec((ctx_keys, dk), lambda b, h, i: (b, h // group)),
        pl.BlockSpec((HEAD_VX, ctx_keys), lambda b, h, i: (h // group, b)),
    ]
    operands += list(kv_ctx)
    return pl.pallas_call(
        functools.partial(_attn_kernel, sub=sub, ck=ck,
                          lat_keys=lat_keys if kv_lat is not None else 0, ctx_keys=ctx_keys),
        grid=(batch, heads, q_tiles),
        in_specs=in_specs,
        out_specs=pl.BlockSpec((bq, HEAD_V), lambda b, h, i: (b * q_tiles + i, h)),
        out_shape=jax.ShapeDtypeStruct((mq, heads * HEAD_V), BF16),
        scratch_shapes=[pltpu.VMEM((bq // sub, n_keys, sub), F32)],
        compiler_params=_params("arbitrary", "arbitrary", "arbitrary"),
        name="attention",
    )(*operands)


def _conv_kernel(cb_ref, cc_ref, cu_ref, w_ref, o_ref):
    t = cc_ref[...].astype(F32) * cu_ref[...].astype(F32)
    n = t.shape[0]
    row = lax.broadcasted_iota(jnp.int32, t.shape, 0)
    t_prev = jnp.where(row == 0, 0.0, pltpu.roll(t, 1, 0))
    t_next = jnp.where(row == n - 1, 0.0, pltpu.roll(t, n - 1, 0))
    w = w_ref[...]
    conv = t_prev * w[0:1, :] + t * w[1:2, :] + t_next * w[2:3, :]
    o_ref[...] = (cb_ref[...].astype(F32) * conv).astype(o_ref.dtype)


def _short_conv_gate(p_conv, conv_w, *, batch):
    m = p_conv.shape[0]
    n = m // batch
    cblk = CONV_WIDTH // LANES
    return pl.pallas_call(
        _conv_kernel,
        grid=(batch, cblk),
        in_specs=[
            pl.BlockSpec((n, LANES), lambda b, j: (b, j)),
            pl.BlockSpec((n, LANES), lambda b, j: (b, cblk + j)),
            pl.BlockSpec((n, LANES), lambda b, j: (b, 2 * cblk + j)),
            pl.BlockSpec((3, LANES), lambda b, j: (0, j)),
        ],
        out_specs=pl.BlockSpec((n, LANES), lambda b, j: (b, j)),
        out_shape=jax.ShapeDtypeStruct((m, CONV_WIDTH), BF16),
        compiler_params=_params("arbitrary", "arbitrary"),
        name="short_conv",
    )(p_conv, p_conv, p_conv, conv_w)


def _out_kernel(*refs, n_parts):
    parts = refs[:n_parts]
    g_ref, w_ref, x_ref, gate_ref, lng_ref, lnb_ref, o_ref = refs[n_parts:]
    y = None
    off = 0
    for p_ref in parts:
        kp = p_ref.shape[1]
        u = p_ref[...] * g_ref[:, off:off + kp]
        part = jnp.dot(u, w_ref[off:off + kp, :], preferred_element_type=F32)
        y = part if y is None else y + part
        off += kp
    z = ALPHA * x_ref[...] + gate_ref[0] * y
    mu = jnp.mean(z, axis=-1, keepdims=True)
    zc = z - mu
    var = jnp.mean(zc * zc, axis=-1, keepdims=True)
    o_ref[...] = zc * lax.rsqrt(var + EPS) * lng_ref[...] + lnb_ref[...]


def _out_proj_norm(parts, g, w, x, mod3, ln_g, ln_b, *, rows_per_mod, mod_row0, tm):
    m, d = x.shape
    assert m % tm == 0
    if rows_per_mod is None:
        mod_idx = lambda i: mod_row0
    else:
        tiles_per_mod = rows_per_mod // tm
        mod_idx = lambda i: mod_row0 + i // tiles_per_mod
    in_specs = [pl.BlockSpec((tm, p.shape[1]), lambda i: (i, 0)) for p in parts]
    in_specs += [
        pl.BlockSpec((tm, g.shape[1]), lambda i: (i, 0)),
        pl.BlockSpec(w.shape, lambda i: (0, 0)),
        pl.BlockSpec((tm, d), lambda i: (i, 0)),
        pl.BlockSpec((1, 1, d), lambda i: (mod_idx(i), 0, 2)),
        pl.BlockSpec((1, d), lambda i: (0, 0)),
        pl.BlockSpec((1, d), lambda i: (0, 0)),
    ]
    return pl.pallas_call(
        functools.partial(_out_kernel, n_parts=len(parts)),
        grid=(m // tm,),
        in_specs=in_specs,
        out_specs=pl.BlockSpec((tm, d), lambda i: (i, 0)),
        out_shape=jax.ShapeDtypeStruct((m, d), F32),
        compiler_params=_params("arbitrary"),
        name="out_proj_norm",
    )(*parts, g, w, x, mod3, ln_g, ln_b)


def _rope_tables(n, ctx_len):
    rows = n // GRID_W
    pos_row = np.repeat(np.arange(rows, dtype=np.float32), GRID_W)
    pos_col = np.tile(np.arange(GRID_W, dtype=np.float32), rows)

    def angles(pos, half):
        inv = jnp.asarray(ROPE_THETA, F32) ** (-jnp.arange(half, dtype=F32) / half)
        return jnp.asarray(pos)[:, None] * inv[None, :]

    ar, ac = angles(pos_row, MLA_ROPE // 4), angles(pos_col, MLA_ROPE // 4)
    cs_mla = jnp.concatenate([jnp.cos(ar), jnp.cos(ar), jnp.cos(ac), jnp.cos(ac),
                              -jnp.sin(ar), jnp.sin(ar), -jnp.sin(ac), jnp.sin(ac)], axis=1)
    cs_mla_ctx = jnp.concatenate([jnp.ones((ctx_len, MLA_ROPE), F32), jnp.zeros((ctx_len, MLA_ROPE), F32)], axis=1)
    br, bc = angles(pos_row, GQA_HEAD_DIM // 4), angles(pos_col, GQA_HEAD_DIM // 4)
    cos_gqa = jnp.concatenate([jnp.cos(br), jnp.cos(bc), jnp.cos(br), jnp.cos(bc)], axis=1)
    sin_gqa = jnp.concatenate([-jnp.sin(br), -jnp.sin(bc), jnp.sin(br), jnp.sin(bc)], axis=1)
    cos_ctx = jnp.ones((ctx_len, GQA_HEAD_DIM), F32)
    sin_ctx = jnp.zeros((ctx_len, GQA_HEAD_DIM), F32)
    return cs_mla, cs_mla_ctx, cos_gqa, sin_gqa, cos_ctx, sin_ctx


def kernel(x, c, ctx, c_ctx, w_mod, b_mod, ln_g, ln_b, a_w_in, a_conv_w, a_q_norm, a_w_qb, a_kv_norm,
           a_w_kvb, a_w_out, c_w_in, c_q_norm, c_k_norm, c_w_out):
    batch, n, d = x.shape
    ctx_len = ctx.shape[1]
    assert d == D_MODEL and batch + 1 <= MOD_ROWS
    ctx_mod_row = batch

    cs_mla, cs_mla_ctx, cos_gqa, sin_gqa, cos_ctx, sin_ctx = _rope_tables(n, ctx_len)

    cvec = jnp.concatenate([c, c_ctx[None, :], jnp.zeros((MOD_ROWS - batch - 1, d), F32)], axis=0)
    mods = _mod_vectors(cvec, w_mod, b_mod)
    mod0 = mods[0].reshape(MOD_ROWS, 1, 3 * d)
    mod1 = mods[1].reshape(MOD_ROWS, 1, 3 * d)

    xl = x.reshape(batch * n, d)
    xc = ctx.reshape(batch * ctx_len, d)

    w_in0 = a_w_in[0]
    o_c = 3 * CONV_WIDTH
    o_q = o_c + MLA_Q_RANK
    o_kv = o_q + MLA_KV_RANK
    o_g = o_kv + MLA_ROPE
    swap16 = np.concatenate([np.arange(16, 32), np.arange(0, 16), np.arange(48, 64), np.arange(32, 48)])
    w_conv = w_in0[:, :o_c].astype(BF16)
    w_rope = w_in0[:, o_kv:o_g]
    w_lat = jnp.concatenate([w_in0[:, o_c:o_kv], w_rope, w_rope[:, swap16]], axis=1).astype(BF16)
    w_gate0 = w_in0[:, o_g:].astype(BF16)
    wq3 = a_w_qb[0].reshape(MLA_Q_RANK, MLA_HEADS, MLA_NOPE + MLA_ROPE)
    wq_rope = wq3[:, :, MLA_NOPE:]
    wq = jnp.concatenate([wq3[:, :, :MLA_NOPE], wq_rope, wq_rope[:, :, swap16]], axis=2)
    wq = wq.reshape(MLA_Q_RANK, MLA_HEADS * MLA_QK_PAD).astype(BF16)
    wkv3 = a_w_kvb[0].reshape(MLA_KV_RANK, MLA_HEADS, MLA_NOPE + MLA_V)
    wkv = jnp.concatenate([wkv3[:, :, :MLA_NOPE].reshape(MLA_KV_RANK, -1),
                           wkv3[:, :, MLA_NOPE:].reshape(MLA_KV_RANK, -1)], axis=1).astype(BF16)
    w_out0 = a_w_out[0].astype(BF16)

    w_in1 = c_w_in[0]
    nq1 = GQA_HEADS * GQA_HEAD_DIM
    nk1 = GQA_KV_HEADS * GQA_HEAD_DIM
    perm = np.concatenate([np.arange(0, 32), np.arange(64, 96), np.arange(32, 64), np.arange(96, 128)])
    w_q1 = w_in1[:, :nq1].reshape(d, GQA_HEADS, GQA_HEAD_DIM)[:, :, perm].reshape(d, nq1).astype(BF16)
    w_k1 = w_in1[:, nq1:nq1 + nk1].reshape(d, GQA_KV_HEADS, GQA_HEAD_DIM)[:, :, perm].reshape(d, nk1)
    w_kv1 = jnp.concatenate([w_k1, w_in1[:, nq1 + nk1:nq1 + 2 * nk1]], axis=1).astype(BF16)
    w_gate1 = w_in1[:, nq1 + 2 * nk1:].astype(BF16)
    q_gain1 = c_q_norm[0][perm].reshape(1, GQA_HEAD_DIM)
    k_gain1 = c_k_norm[0][perm].reshape(1, GQA_HEAD_DIM)
    w_out1 = c_w_out[0].astype(BF16)

    lng0, lnb0 = ln_g[0].reshape(1, d), ln_b[0].reshape(1, d)
    lng1, lnb1 = ln_g[1].reshape(1, d), ln_b[1].reshape(1, d)

    lat = dict(rows_per_mod=n, mod_row0=0)
    cxt = dict(rows_per_mod=None, mod_row0=ctx_mod_row)
    tm_l, tm_c = 1024, ctx_len
    q_scale0 = (MLA_NOPE + MLA_ROPE) ** -0.5 * LOG2E
    q_scale1 = GQA_HEAD_DIM ** -0.5 * LOG2E

    def layer0_front(rows, mod_kw, tm_proj, cs_tab):
        p_conv = _mod_matmul(rows, mod0, w_conv, tm=tm_l, tn=1024, **mod_kw)
        p_lat = _mod_matmul(rows, mod0, w_lat, tm=tm_l, tn=w_lat.shape[1], **mod_kw)
        gate = _mod_matmul(rows, mod0, w_gate0, tm=tm_l, tn=1024, epilogue="silu", **mod_kw)
        qt, k, vt = _mla_proj(p_lat, a_q_norm[0].reshape(1, -1), a_kv_norm[0].reshape(1, -1), wq, wkv,
                              cs_tab, tm=tm_proj, q_scale=q_scale0)
        return p_conv, gate, qt, k, vt

    pcl, gl, qtl, kl, vtl = layer0_front(xl, lat, 512, cs_mla)
    pcc, gc, qtc, kc, vtc = layer0_front(xc, cxt, tm_c, cs_mla_ctx)

    mla = dict(batch=batch, heads=MLA_HEADS, group=1, dk=MLA_QK_PAD, lat_keys=n, ctx_keys=ctx_len)
    attn_l = _attention(qtl, (kl, vtl), (kc, vtc), bq=1024, sub=512, **mla)
    attn_c = _attention(qtc, None, (kc, vtc), bq=ctx_len, sub=ctx_len, **mla)

    conv_l = _short_conv_gate(pcl, a_conv_w[0], batch=batch)
    conv_c = _short_conv_gate(pcc, a_conv_w[0], batch=batch)

    x1 = _out_proj_norm([conv_l, attn_l], gl, w_out0, xl, mod0, lng0, lnb0, tm=256, **lat)
    h1 = _out_proj_norm([conv_c, attn_c], gc, w_out0, xc, mod0, lng0, lnb0, tm=256, **cxt)

    qt1 = _mod_matmul(x1, mod1, w_q1, tm=tm_l, tn=1024, epilogue="q_rope", q_scale=q_scale1,
                      extra=(q_gain1, cos_gqa, sin_gqa), pos_rows=n, **lat)
    k1, vt1 = _mod_matmul(x1, mod1, w_kv1, tm=tm_l, tn=2 * nk1, epilogue="kv_rope",
                          extra=(k_gain1, cos_gqa, sin_gqa), pos_rows=n, **lat)
    g1 = _mod_matmul(x1, mod1, w_gate1, tm=tm_l, tn=1024, epilogue="silu", **lat)
    kc1, vtc1 = _mod_matmul(h1, mod1, w_kv1, tm=tm_c, tn=2 * nk1, epilogue="kv_rope",
                            extra=(k_gain1, cos_ctx, sin_ctx), pos_rows=ctx_len, **cxt)

    attn1 = _attention(qt1, (k1, vt1), (kc1, vtc1), batch=batch, heads=GQA_HEADS, group=GQA_GROUP,
                       dk=GQA_HEAD_DIM, bq=1024, sub=512, lat_keys=n, ctx_keys=ctx_len)

    out = _out_proj_norm([attn1], g1, w_out1, x1, mod1, lng1, lnb1, tm=256, **lat)
    return out.reshape(batch, n, d)
```

```python
import functools
import math

import jax
import jax.numpy as jnp
import numpy as np
from jax import lax
from jax.experimental import pallas as pl
from jax.experimental.pallas import tpu as pltpu

F32 = jnp.float32
BF16 = jnp.bfloat16

D_MODEL = 2048
DEPTH = 2
GRID_W = 64
ROPE_THETA = 10000.0
EPS = 1e-6
ALPHA = (2 * DEPTH) ** 0.25

CONV_WIDTH = D_MODEL // 2
MLA_HEADS = D_MODEL // 256
MLA_Q_RANK = 512
MLA_KV_RANK = 512
MLA_NOPE = 128
MLA_ROPE = 64
MLA_V = 128
MLA_QK_PAD = 256
GQA_HEAD_DIM = 128
GQA_HEADS = D_MODEL // GQA_HEAD_DIM
GQA_KV_HEADS = GQA_HEADS // 4
GQA_GROUP = GQA_HEADS // GQA_KV_HEADS

LANES = 128
BF16_SUBLANES = 16
HEAD_V = 128
HEAD_VX = HEAD_V + BF16_SUBLANES
MOD_ROWS = 16
VMEM_LIMIT = 56 * 1024 * 1024
LOG2E = math.log2(math.e)
ATTN_KEY_BLOCKS = 2
ATTN_CHUNK = 256


def _params(*sem):
    return pltpu.CompilerParams(dimension_semantics=sem, vmem_limit_bytes=VMEM_LIMIT)


def _silu(v):
    return v * jax.nn.sigmoid(v)


_NT_DIMS = (((1,), (1,)), ((), ()))


def _dot_nt(a, b):
    return lax.dot_general(a, b, _NT_DIMS, preferred_element_type=F32)


def _store_values_t(vt_ref, vt, n_heads):
    rows = vt.shape[1]
    for h in range(n_heads):
        lo = h * HEAD_VX
        vt_ref[lo:lo + HEAD_V, :] = vt[h * HEAD_V:(h + 1) * HEAD_V].astype(vt_ref.dtype)
        vt_ref[lo + HEAD_V:lo + HEAD_VX, :] = jnp.ones((BF16_SUBLANES, rows), vt_ref.dtype)


def _mod_kernel(c_ref, w_ref, b_ref, o_ref):
    s = _silu(c_ref[...]).astype(BF16)
    o_ref[0] = jnp.dot(s, w_ref[0].astype(BF16), preferred_element_type=F32) + b_ref[0]


def _mod_vectors(cvec, w_mod, b_mod):
    depth, d, n3 = w_mod.shape
    tn = 768
    return pl.pallas_call(
        _mod_kernel,
        grid=(depth, n3 // tn),
        in_specs=[
            pl.BlockSpec((MOD_ROWS, d), lambda l, j: (0, 0)),
            pl.BlockSpec((1, d, tn), lambda l, j: (l, 0, j)),
            pl.BlockSpec((1, 1, tn), lambda l, j: (l, 0, j)),
        ],
        out_specs=pl.BlockSpec((1, MOD_ROWS, tn), lambda l, j: (l, 0, j)),
        out_shape=jax.ShapeDtypeStruct((depth, MOD_ROWS, n3), F32),
        compiler_params=_params("arbitrary", "arbitrary"),
        name="mod_vectors",
    )(cvec, w_mod, b_mod.reshape(depth, 1, n3))


def _rms_rope(y, g, cos, sin):
    ms = jnp.mean(y * y, axis=-1, keepdims=True)
    yn = y * lax.rsqrt(ms + EPS) * g
    return yn * cos + pltpu.roll(yn, LANES // 2, 1) * sin


def _modulate(x_ref, sh_ref, sc_ref, xin_ref):
    @pl.when(pl.program_id(1) == 0)
    def _():
        xin_ref[...] = (x_ref[...] * (1.0 + sc_ref[0]) + sh_ref[0]).astype(BF16)


def _plain_kernel(x_ref, sh_ref, sc_ref, w_ref, o_ref, xin_ref, *, act):
    _modulate(x_ref, sh_ref, sc_ref, xin_ref)
    acc = jnp.dot(xin_ref[...], w_ref[...], preferred_element_type=F32)
    o_ref[...] = (_silu(acc) if act == "silu" else acc).astype(o_ref.dtype)


def _q_heads_kernel(x_ref, sh_ref, sc_ref, wt_ref, g_ref, gp_ref, cos_ref, sin_ref, qt_ref, xin_ref, *, q_scale):
    _modulate(x_ref, sh_ref, sc_ref, xin_ref)
    acc = _dot_nt(wt_ref[...], xin_ref[...])
    a = g_ref[...] * cos_ref[...]
    b = gp_ref[...] * sin_ref[...]
    half = LANES // 2
    for h in range(acc.shape[0] // LANES):
        y = acc[h * LANES:(h + 1) * LANES]
        yn = y * (lax.rsqrt(jnp.mean(y * y, axis=0, keepdims=True) + EPS) * q_scale)
        partner = jnp.concatenate([yn[half:], yn[:half]], axis=0)
        qt_ref[h * LANES:(h + 1) * LANES, :] = (yn * a + partner * b).astype(qt_ref.dtype)


def _kv_heads_kernel(x_ref, sh_ref, sc_ref, wk_ref, wvt_ref, g_ref, cos_ref, sin_ref, k_ref, vt_ref, xin_ref):
    _modulate(x_ref, sh_ref, sc_ref, xin_ref)
    xin = xin_ref[...]
    g, cos, sin = g_ref[...], cos_ref[...], sin_ref[...]
    yk = jnp.dot(xin, wk_ref[...], preferred_element_type=F32)
    for h in range(yk.shape[1] // LANES):
        k_ref[:, h * LANES:(h + 1) * LANES] = _rms_rope(
            yk[:, h * LANES:(h + 1) * LANES], g, cos, sin).astype(k_ref.dtype)
    _store_values_t(vt_ref, _dot_nt(wvt_ref[...], xin), wvt_ref.shape[0] // HEAD_V)


def _mod_specs(d, tm, rows_per_mod, mod_row0):
    if rows_per_mod is None:
        mod_idx = lambda i: mod_row0
    else:
        assert rows_per_mod % tm == 0
        tiles_per_mod = rows_per_mod // tm
        mod_idx = lambda i: mod_row0 + i // tiles_per_mod
    return [
        pl.BlockSpec((tm, d), lambda i, j: (i, 0)),
        pl.BlockSpec((1, 1, d), lambda i, j: (mod_idx(i), 0, 0)),
        pl.BlockSpec((1, 1, d), lambda i, j: (mod_idx(i), 0, 1)),
    ]


def _mod_matmul(x, mod3, w, *, rows_per_mod, mod_row0, tm, tn, act="none"):
    m, d = x.shape
    n = w.shape[1]
    assert m % tm == 0 and n % tn == 0
    return pl.pallas_call(
        functools.partial(_plain_kernel, act=act),
        grid=(m // tm, n // tn),
        in_specs=_mod_specs(d, tm, rows_per_mod, mod_row0) + [pl.BlockSpec((d, tn), lambda i, j: (0, j))],
        out_specs=pl.BlockSpec((tm, tn), lambda i, j: (i, j)),
        out_shape=jax.ShapeDtypeStruct((m, n), BF16),
        scratch_shapes=[pltpu.VMEM((tm, d), BF16)],
        compiler_params=_params("arbitrary", "arbitrary"),
        name="mod_matmul_" + act,
    )(x, mod3, mod3, w)


def _q_heads(x, mod3, wt, gain, gain_partner, cos_t, sin_t, *, rows_per_mod, mod_row0, tm, tn, q_scale):
    m, d = x.shape
    n = wt.shape[0]
    pos = cos_t.shape[1]
    assert m % tm == 0 and n % tn == 0 and pos % tm == 0
    tiles_per_seq = pos // tm
    col = pl.BlockSpec((LANES, 1), lambda i, j: (0, 0))
    tab = pl.BlockSpec((LANES, tm), lambda i, j: (0, i % tiles_per_seq))
    return pl.pallas_call(
        functools.partial(_q_heads_kernel, q_scale=q_scale),
        grid=(m // tm, n // tn),
        in_specs=_mod_specs(d, tm, rows_per_mod, mod_row0) + [
            pl.BlockSpec((tn, d), lambda i, j: (j, 0)), col, col, tab, tab],
        out_specs=pl.BlockSpec((tn, tm), lambda i, j: (j, i)),
        out_shape=jax.ShapeDtypeStruct((n, m), BF16),
        scratch_shapes=[pltpu.VMEM((tm, d), BF16)],
        compiler_params=_params("arbitrary", "arbitrary"),
        name="q_heads",
    )(x, mod3, mod3, wt, gain, gain_partner, cos_t, sin_t)


def _kv_heads(x, mod3, wk, wvt, gain, cos, sin, *, rows_per_mod, mod_row0, tm):
    m, d = x.shape
    nk = wk.shape[1]
    nv = wvt.shape[0]
    pos = cos.shape[0]
    assert m % tm == 0 and pos % tm == 0
    tiles_per_seq = pos // tm
    vrows = (nv // HEAD_V) * HEAD_VX
    tab = pl.BlockSpec((tm, LANES), lambda i, j: (i % tiles_per_seq, 0))
    return pl.pallas_call(
        _kv_heads_kernel,
        grid=(m // tm, 1),
        in_specs=_mod_specs(d, tm, rows_per_mod, mod_row0) + [
            pl.BlockSpec((d, nk), lambda i, j: (0, 0)),
            pl.BlockSpec((nv, d), lambda i, j: (0, 0)),
            pl.BlockSpec((1, LANES), lambda i, j: (0, 0)), tab, tab],
        out_specs=[pl.BlockSpec((tm, nk), lambda i, j: (i, 0)), pl.BlockSpec((vrows, tm), lambda i, j: (0, i))],
        out_shape=[jax.ShapeDtypeStruct((m, nk), BF16), jax.ShapeDtypeStruct((vrows, m), BF16)],
        scratch_shapes=[pltpu.VMEM((tm, d), BF16)],
        compiler_params=_params("arbitrary", "arbitrary"),
        name="kv_heads",
    )(x, mod3, mod3, wk, wvt, gain, cos, sin)


def _mla_proj_kernel(ql_ref, kvl_ref, kr_ref, qg_ref, kvg_ref, wqt_ref, wk_ref, wvt_ref, cs_ref, cst_ref,
                     qt_ref, k_ref, vt_ref, *, q_scale):
    def rms(v, g):
        vf = v.astype(F32)
        return (vf * lax.rsqrt(jnp.mean(vf * vf, axis=-1, keepdims=True) + EPS) * g).astype(BF16)

    yqt = _dot_nt(wqt_ref[...], rms(ql_ref[...], qg_ref[...]))
    cst = cst_ref[...]
    for h in range(MLA_HEADS):
        lo = h * MLA_QK_PAD
        qt_ref[lo:lo + MLA_NOPE, :] = (yqt[lo:lo + MLA_NOPE] * q_scale).astype(qt_ref.dtype)
        t = yqt[lo + MLA_NOPE:lo + MLA_QK_PAD] * cst
        rope = (t[:MLA_ROPE] + t[MLA_ROPE:]) * q_scale
        qt_ref[lo + MLA_NOPE:lo + MLA_NOPE + MLA_ROPE, :] = rope.astype(qt_ref.dtype)
        qt_ref[lo + MLA_NOPE + MLA_ROPE:lo + MLA_QK_PAD, :] = jnp.zeros((MLA_ROPE, rope.shape[1]), qt_ref.dtype)

    rkv = rms(kvl_ref[...], kvg_ref[...])
    yk = jnp.dot(rkv, wk_ref[...], preferred_element_type=F32)
    t = kr_ref[...].astype(F32) * cs_ref[...]
    kr = t + pltpu.roll(t, LANES // 2, 1)
    lane = lax.broadcasted_iota(jnp.int32, kr.shape, 1)
    kr = jnp.where(lane < MLA_ROPE, kr, 0.0).astype(k_ref.dtype)
    for h in range(MLA_HEADS):
        lo = h * MLA_QK_PAD
        k_ref[:, lo:lo + MLA_NOPE] = yk[:, h * MLA_NOPE:(h + 1) * MLA_NOPE].astype(k_ref.dtype)
        k_ref[:, lo + MLA_NOPE:lo + MLA_QK_PAD] = kr
    _store_values_t(vt_ref, _dot_nt(wvt_ref[...], rkv), MLA_HEADS)


def _mla_proj(p_lat, q_gain, kv_gain, wqt, wk, wvt, cs_tab, cs_tab_t, *, tm, q_scale):
    m = p_lat.shape[0]
    pos_rows = cs_tab.shape[0]
    assert m % tm == 0 and pos_rows % tm == 0
    tiles_per_seq = pos_rows // tm
    hq = MLA_HEADS * MLA_QK_PAD
    hv = MLA_HEADS * MLA_V
    hvx = MLA_HEADS * HEAD_VX
    rope_blk = (MLA_Q_RANK + MLA_KV_RANK) // LANES
    return pl.pallas_call(
        functools.partial(_mla_proj_kernel, q_scale=q_scale),
        grid=(m // tm,),
        in_specs=[
            pl.BlockSpec((tm, MLA_Q_RANK), lambda i: (i, 0)),
            pl.BlockSpec((tm, MLA_KV_RANK), lambda i: (i, 1)),
            pl.BlockSpec((tm, LANES), lambda i: (i, rope_blk)),
            pl.BlockSpec((1, MLA_Q_RANK), lambda i: (0, 0)),
            pl.BlockSpec((1, MLA_KV_RANK), lambda i: (0, 0)),
            pl.BlockSpec((hq, MLA_Q_RANK), lambda i: (0, 0)),
            pl.BlockSpec((MLA_KV_RANK, hv), lambda i: (0, 0)),
            pl.BlockSpec((hv, MLA_KV_RANK), lambda i: (0, 0)),
            pl.BlockSpec((tm, LANES), lambda i: (i % tiles_per_seq, 0)),
            pl.BlockSpec((LANES, tm), lambda i: (0, i % tiles_per_seq)),
        ],
        out_specs=[
            pl.BlockSpec((hq, tm), lambda i: (0, i)),
            pl.BlockSpec((tm, hq), lambda i: (i, 0)),
            pl.BlockSpec((hvx, tm), lambda i: (0, i)),
        ],
        out_shape=[
            jax.ShapeDtypeStruct((hq, m), BF16),
            jax.ShapeDtypeStruct((m, hq), BF16),
            jax.ShapeDtypeStruct((hvx, m), BF16),
        ],
        compiler_params=_params("arbitrary"),
        name="mla_proj",
    )(p_lat, p_lat, p_lat, q_gain, kv_gain, wqt, wk, wvt, cs_tab, cs_tab_t)


def _attn_kernel(*refs, sub, lat_keys, ctx_keys):
    if lat_keys:
        qt_ref, kl_ref, vtl_ref, kc_ref, vtc_ref, o_ref, s_a, s_b, acc_scr = refs
    else:
        qt_ref, kc_ref, vtc_ref, o_ref, s_a, s_b, acc_scr = refs
        kl_ref = vtl_ref = None
    n_sub = qt_ref.shape[1] // sub
    n_it = ATTN_KEY_BLOCKS if lat_keys else 1
    lat_it = lat_keys // n_it
    ctx_it = ctx_keys // n_it

    def scores(slot, qt, k_rows, row0, rows):
        s = jnp.dot(k_rows, qt, preferred_element_type=F32)
        slot[pl.ds(row0, rows), :] = s
        return jnp.max(s.reshape(rows // 8, 8, sub), axis=0)

    def weighted(slot, m, vt_cols, row0, rows):
        p = jnp.exp2(slot[pl.ds(row0, rows), :] - m).astype(BF16)
        return jnp.dot(vt_cols, p, preferred_element_type=F32)

    def stage(j1, j2, m2):
        qt = None if j1 is None else qt_ref[:, j1 * sub:(j1 + 1) * sub]
        slot1 = None if j1 is None else (s_a, s_b)[j1 % 2]
        slot2 = None if j2 is None else (s_a, s_b)[j2 % 2]
        mrun = jnp.full((8, sub), -jnp.inf, F32)
        if j2 is not None:
            acc_scr[...] = jnp.zeros(acc_scr.shape, F32)

        def body(it, mrun):
            pieces = []
            if lat_it:
                l0 = it * lat_it
                pieces += [(kl_ref, vtl_ref, l0 + c * ATTN_CHUNK, l0 + c * ATTN_CHUNK, ATTN_CHUNK)
                           for c in range(lat_it // ATTN_CHUNK)]
            c0 = it * ctx_it
            pieces.append((kc_ref, vtc_ref, c0, lat_keys + c0, ctx_it))
            part = None
            for k_ref, vt_ref, src0, dst0, rows in pieces:
                if j2 is not None:
                    d = weighted(slot2, m2, vt_ref[:, pl.ds(src0, rows)], dst0, rows)
                    part = d if part is None else part + d
                if j1 is not None:
                    mrun = jnp.maximum(mrun, scores(slot1, qt, k_ref[pl.ds(src0, rows), :], dst0, rows))
            if j2 is not None:
                acc_scr[...] += part
            return mrun

        mrun = body(0, mrun) if n_it == 1 else lax.fori_loop(0, n_it, body, mrun)
        m1 = None
        if j1 is not None:
            m1 = jnp.max(mrun, axis=0, keepdims=True)
        if j2 is not None:
            acc = acc_scr[...]
            ot = acc[:HEAD_V] / acc[HEAD_V:HEAD_V + 1]
            o_ref[j2 * sub:(j2 + 1) * sub, :] = ot.T.astype(o_ref.dtype)
        return m1

    m_prev = stage(0, None, None)
    for j in range(1, n_sub):
        m_prev = stage(j, j - 1, m_prev)
    stage(None, n_sub - 1, m_prev)


def _attention(qt, kv_lat, kv_ctx, *, batch, heads, group, dk, bq, sub, lat_keys, ctx_keys):
    mq = qt.shape[1]
    nq = mq // batch
    assert nq % bq == 0 and bq % sub == 0
    q_tiles = nq // bq
    in_specs = [pl.BlockSpec((dk, bq), lambda b, h, i: (h, b * q_tiles + i))]
    operands = [qt]
    n_keys = ctx_keys
    if kv_lat is not None:
        assert lat_keys % (ATTN_KEY_BLOCKS * ATTN_CHUNK) == 0 and ctx_keys % (ATTN_KEY_BLOCKS * LANES) == 0
        n_keys += lat_keys
        in_specs += [
            pl.BlockSpec((lat_keys, dk), lambda b, h, i: (b, h // group)),
            pl.BlockSpec((HEAD_VX, lat_keys), lambda b, h, i: (h // group, b)),
        ]
        operands += list(kv_lat)
    in_specs += [
        pl.BlockSpec((ctx_keys, dk), lambda b, h, i: (b, h // group)),
        pl.BlockSpec((HEAD_VX, ctx_keys), lambda b, h, i: (h // group, b)),
    ]
    operands += list(kv_ctx)
    return pl.pallas_call(
        functools.partial(_attn_kernel, sub=sub,
                          lat_keys=lat_keys if kv_lat is not None else 0, ctx_keys=ctx_keys),
        grid=(batch, heads, q_tiles),
        in_specs=in_specs,
        out_specs=pl.BlockSpec((bq, HEAD_V), lambda b, h, i: (b * q_tiles + i, h)),
        out_shape=jax.ShapeDtypeStruct((mq, heads * HEAD_V), BF16),
        scratch_shapes=[pltpu.VMEM((n_keys, sub), F32), pltpu.VMEM((n_keys, sub), F32),
                        pltpu.VMEM((HEAD_VX, sub), F32)],
        compiler_params=_params("arbitrary", "arbitrary", "arbitrary"),
        name="attention",
    )(*operands)


def _conv_kernel(cb_ref, cc_ref, cu_ref, w_ref, o_ref):
    t = cc_ref[...].astype(F32) * cu_ref[...].astype(F32)
    n = t.shape[0]
    row = lax.broadcasted_iota(jnp.int32, t.shape, 0)
    t_prev = jnp.where(row == 0, 0.0, pltpu.roll(t, 1, 0))
    t_next = jnp.where(row == n - 1, 0.0, pltpu.roll(t, n - 1, 0))
    w = w_ref[...]
    conv = t_prev * w[0:1, :] + t * w[1:2, :] + t_next * w[2:3, :]
    o_ref[...] = (cb_ref[...].astype(F32) * conv).astype(o_ref.dtype)


def _short_conv_gate(p_conv, conv_w, *, batch):
    m = p_conv.shape[0]
    n = m // batch
    cblk = CONV_WIDTH // LANES
    return pl.pallas_call(
        _conv_kernel,
        grid=(batch, cblk),
        in_specs=[
            pl.BlockSpec((n, LANES), lambda b, j: (b, j)),
            pl.BlockSpec((n, LANES), lambda b, j: (b, cblk + j)),
            pl.BlockSpec((n, LANES), lambda b, j: (b, 2 * cblk + j)),
            pl.BlockSpec((3, LANES), lambda b, j: (0, j)),
        ],
        out_specs=pl.BlockSpec((n, LANES), lambda b, j: (b, j)),
        out_shape=jax.ShapeDtypeStruct((m, CONV_WIDTH), BF16),
        compiler_params=_params("arbitrary", "arbitrary"),
        name="short_conv",
    )(p_conv, p_conv, p_conv, conv_w)


def _out_kernel(*refs, n_parts):
    parts = refs[:n_parts]
    g_ref, w_ref, x_ref, gate_ref, lng_ref, lnb_ref, o_ref = refs[n_parts:]
    y = None
    off = 0
    for p_ref in parts:
        kp = p_ref.shape[1]
        u = p_ref[...] * g_ref[:, off:off + kp]
        part = jnp.dot(u, w_ref[off:off + kp, :], preferred_element_type=F32)
        y = part if y is None else y + part
        off += kp
    z = ALPHA * x_ref[...] + gate_ref[0] * y
    mu = jnp.mean(z, axis=-1, keepdims=True)
    zc = z - mu
    var = jnp.mean(zc * zc, axis=-1, keepdims=True)
    o_ref[...] = zc * lax.rsqrt(var + EPS) * lng_ref[...] + lnb_ref[...]


def _out_proj_norm(parts, g, w, x, mod3, ln_g, ln_b, *, rows_per_mod, mod_row0, tm):
    m, d = x.shape
    assert m % tm == 0
    if rows_per_mod is None:
        mod_idx = lambda i: mod_row0
    else:
        tiles_per_mod = rows_per_mod // tm
        mod_idx = lambda i: mod_row0 + i // tiles_per_mod
    in_specs = [pl.BlockSpec((tm, p.shape[1]), lambda i: (i, 0)) for p in parts]
    in_specs += [
        pl.BlockSpec((tm, g.shape[1]), lambda i: (i, 0)),
        pl.BlockSpec(w.shape, lambda i: (0, 0)),
        pl.BlockSpec((tm, d), lambda i: (i, 0)),
        pl.BlockSpec((1, 1, d), lambda i: (mod_idx(i), 0, 2)),
        pl.BlockSpec((1, d), lambda i: (0, 0)),
        pl.BlockSpec((1, d), lambda i: (0, 0)),
    ]
    return pl.pallas_call(
        functools.partial(_out_kernel, n_parts=len(parts)),
        grid=(m // tm,),
        in_specs=in_specs,
        out_specs=pl.BlockSpec((tm, d), lambda i: (i, 0)),
        out_shape=jax.ShapeDtypeStruct((m, d), F32),
        compiler_params=_params("arbitrary"),
        name="out_proj_norm",
    )(*parts, g, w, x, mod3, ln_g, ln_b)


def _rope_tables(n, ctx_len):
    rows = n // GRID_W
    pos_row = np.repeat(np.arange(rows, dtype=np.float32), GRID_W)
    pos_col = np.tile(np.arange(GRID_W, dtype=np.float32), rows)

    def angles(pos, half):
        inv = jnp.asarray(ROPE_THETA, F32) ** (-jnp.arange(half, dtype=F32) / half)
        return jnp.asarray(pos)[:, None] * inv[None, :]

    ar, ac = angles(pos_row, MLA_ROPE // 4), angles(pos_col, MLA_ROPE // 4)
    cs_mla = jnp.concatenate([jnp.cos(ar), jnp.cos(ar), jnp.cos(ac), jnp.cos(ac),
                              -jnp.sin(ar), jnp.sin(ar), -jnp.sin(ac), jnp.sin(ac)], axis=1)
    cs_mla_ctx = jnp.concatenate([jnp.ones((ctx_len, MLA_ROPE), F32), jnp.zeros((ctx_len, MLA_ROPE), F32)], axis=1)
    br, bc = angles(pos_row, GQA_HEAD_DIM // 4), angles(pos_col, GQA_HEAD_DIM // 4)
    cos_gqa = jnp.concatenate([jnp.cos(br), jnp.cos(bc), jnp.cos(br), jnp.cos(bc)], axis=1)
    sin_gqa = jnp.concatenate([-jnp.sin(br), -jnp.sin(bc), jnp.sin(br), jnp.sin(bc)], axis=1)
    cos_ctx = jnp.ones((ctx_len, GQA_HEAD_DIM), F32)
    sin_ctx = jnp.zeros((ctx_len, GQA_HEAD_DIM), F32)
    return dict(mla=cs_mla, mla_ctx=cs_mla_ctx, mla_t=cs_mla.T, mla_ctx_t=cs_mla_ctx.T,
                cos=cos_gqa, sin=sin_gqa, cos_t=cos_gqa.T, sin_t=sin_gqa.T, cos_ctx=cos_ctx, sin_ctx=sin_ctx)


def kernel(x, c, ctx, c_ctx, w_mod, b_mod, ln_g, ln_b, a_w_in, a_conv_w, a_q_norm, a_w_qb, a_kv_norm,
           a_w_kvb, a_w_out, c_w_in, c_q_norm, c_k_norm, c_w_out):
    batch, n, d = x.shape
    ctx_len = ctx.shape[1]
    assert d == D_MODEL and batch + 1 <= MOD_ROWS
    ctx_mod_row = batch

    tabs = _rope_tables(n, ctx_len)

    cvec = jnp.concatenate([c, c_ctx[None, :], jnp.zeros((MOD_ROWS - batch - 1, d), F32)], axis=0)
    mods = _mod_vectors(cvec, w_mod, b_mod)
    mod0 = mods[0].reshape(MOD_ROWS, 1, 3 * d)
    mod1 = mods[1].reshape(MOD_ROWS, 1, 3 * d)

    xl = x.reshape(batch * n, d)
    xc = ctx.reshape(batch * ctx_len, d)

    w_in0 = a_w_in[0]
    o_c = 3 * CONV_WIDTH
    o_q = o_c + MLA_Q_RANK
    o_kv = o_q + MLA_KV_RANK
    o_g = o_kv + MLA_ROPE
    swap16 = np.concatenate([np.arange(16, 32), np.arange(0, 16), np.arange(48, 64), np.arange(32, 48)])
    w_conv = w_in0[:, :o_c].astype(BF16)
    w_rope = w_in0[:, o_kv:o_g]
    w_lat = jnp.concatenate([w_in0[:, o_c:o_kv], w_rope, w_rope[:, swap16]], axis=1).astype(BF16)
    w_gate0 = w_in0[:, o_g:].astype(BF16)
    wq3 = a_w_qb[0].reshape(MLA_Q_RANK, MLA_HEADS, MLA_NOPE + MLA_ROPE)
    wq_rope = wq3[:, :, MLA_NOPE:]
    wq = jnp.concatenate([wq3[:, :, :MLA_NOPE], wq_rope, wq_rope[:, :, swap16]], axis=2)
    wq_t = wq.reshape(MLA_Q_RANK, MLA_HEADS * MLA_QK_PAD).T.astype(BF16)
    wkv3 = a_w_kvb[0].reshape(MLA_KV_RANK, MLA_HEADS, MLA_NOPE + MLA_V)
    wk0 = wkv3[:, :, :MLA_NOPE].reshape(MLA_KV_RANK, -1).astype(BF16)
    wv0_t = wkv3[:, :, MLA_NOPE:].reshape(MLA_KV_RANK, -1).T.astype(BF16)
    w_out0 = a_w_out[0].astype(BF16)

    w_in1 = c_w_in[0]
    nq1 = GQA_HEADS * GQA_HEAD_DIM
    nk1 = GQA_KV_HEADS * GQA_HEAD_DIM
    perm = np.concatenate([np.arange(0, 32), np.arange(64, 96), np.arange(32, 64), np.arange(96, 128)])
    partner = np.concatenate([np.arange(64, 128), np.arange(0, 64)])
    w_q1_t = w_in1[:, :nq1].reshape(d, GQA_HEADS, GQA_HEAD_DIM)[:, :, perm].reshape(d, nq1).T.astype(BF16)
    w_k1 = w_in1[:, nq1:nq1 + nk1].reshape(d, GQA_KV_HEADS, GQA_HEAD_DIM)[:, :, perm].reshape(d, nk1).astype(BF16)
    w_v1_t = w_in1[:, nq1 + nk1:nq1 + 2 * nk1].T.astype(BF16)
    w_gate1 = w_in1[:, nq1 + 2 * nk1:].astype(BF16)
    q_gain1 = c_q_norm[0][perm].reshape(GQA_HEAD_DIM, 1)
    q_gain1_partner = c_q_norm[0][perm][partner].reshape(GQA_HEAD_DIM, 1)
    k_gain1 = c_k_norm[0][perm].reshape(1, GQA_HEAD_DIM)
    w_out1 = c_w_out[0].astype(BF16)

    lng0, lnb0 = ln_g[0].reshape(1, d), ln_b[0].reshape(1, d)
    lng1, lnb1 = ln_g[1].reshape(1, d), ln_b[1].reshape(1, d)

    lat = dict(rows_per_mod=n, mod_row0=0)
    cxt = dict(rows_per_mod=None, mod_row0=ctx_mod_row)
    tm_l, tm_c = 1024, ctx_len
    q_scale0 = (MLA_NOPE + MLA_ROPE) ** -0.5 * LOG2E
    q_scale1 = GQA_HEAD_DIM ** -0.5 * LOG2E

    def layer0_front(rows, mod_kw, tm_proj, cs_tab, cs_tab_t):
        p_conv = _mod_matmul(rows, mod0, w_conv, tm=tm_l, tn=1024, **mod_kw)
        p_lat = _mod_matmul(rows, mod0, w_lat, tm=tm_l, tn=w_lat.shape[1], **mod_kw)
        gate = _mod_matmul(rows, mod0, w_gate0, tm=tm_l, tn=1024, act="silu", **mod_kw)
        qt, k, vt = _mla_proj(p_lat, a_q_norm[0].reshape(1, -1), a_kv_norm[0].reshape(1, -1), wq_t, wk0, wv0_t,
                              cs_tab, cs_tab_t, tm=tm_proj, q_scale=q_scale0)
        return p_conv, gate, qt, k, vt

    pcl, gl, qtl, kl, vtl = layer0_front(xl, lat, 512, tabs["mla"], tabs["mla_t"])
    pcc, gc, qtc, kc, vtc = layer0_front(xc, cxt, tm_c, tabs["mla_ctx"], tabs["mla_ctx_t"])

    mla = dict(batch=batch, heads=MLA_HEADS, group=1, dk=MLA_QK_PAD, lat_keys=n, ctx_keys=ctx_len)
    attn_l = _attention(qtl, (kl, vtl), (kc, vtc), bq=n, sub=512, **mla)
    attn_c = _attention(qtc, None, (kc, vtc), bq=ctx_len, sub=ctx_len, **mla)

    conv_l = _short_conv_gate(pcl, a_conv_w[0], batch=batch)
    conv_c = _short_conv_gate(pcc, a_conv_w[0], batch=batch)

    x1 = _out_proj_norm([conv_l, attn_l], gl, w_out0, xl, mod0, lng0, lnb0, tm=256, **lat)
    h1 = _out_proj_norm([conv_c, attn_c], gc, w_out0, xc, mod0, lng0, lnb0, tm=256, **cxt)

    qt1 = _q_heads(x1, mod1, w_q1_t, q_gain1, q_gain1_partner, tabs["cos_t"], tabs["sin_t"],
                   tm=tm_l, tn=1024, q_scale=q_scale1, **lat)
    k1, vt1 = _kv_heads(x1, mod1, w_k1, w_v1_t, k_gain1, tabs["cos"], tabs["sin"], tm=tm_l, **lat)
    g1 = _mod_matmul(x1, mod1, w_gate1, tm=tm_l, tn=1024, act="silu", **lat)
    kc1, vtc1 = _kv_heads(h1, mod1, w_k1, w_v1_t, k_gain1, tabs["cos_ctx"], tabs["sin_ctx"], tm=tm_c, **cxt)

    attn1 = _attention(qt1, (k1, vt1), (kc1, vtc1), batch=batch, heads=GQA_HEADS, group=GQA_GROUP,
                       dk=GQA_HEAD_DIM, bq=n, sub=512, lat_keys=n, ctx_keys=ctx_len)

    out = _out_proj_norm([attn1], g1, w_out1, x1, mod1, lng1, lnb1, tm=256, **lat)
    return out.reshape(batch, n, d)
```

```python
import functools
import math

import jax
import jax.numpy as jnp
import numpy as np
from jax import lax
from jax.experimental import pallas as pl
from jax.experimental.pallas import tpu as pltpu

F32 = jnp.float32
BF16 = jnp.bfloat16

D_MODEL = 2048
DEPTH = 2
GRID_W = 64
ROPE_THETA = 10000.0
EPS = 1e-6
ALPHA = (2 * DEPTH) ** 0.25

CONV_WIDTH = D_MODEL // 2
MLA_HEADS = D_MODEL // 256
MLA_Q_RANK = 512
MLA_KV_RANK = 512
MLA_NOPE = 128
MLA_ROPE = 64
MLA_V = 128
MLA_QK_PAD = 256
GQA_HEAD_DIM = 128
GQA_HEADS = D_MODEL // GQA_HEAD_DIM
GQA_KV_HEADS = GQA_HEADS // 4
GQA_GROUP = GQA_HEADS // GQA_KV_HEADS

LANES = 128
HEAD_V = 128
MOD_ROWS = 16
VMEM_LIMIT = 56 * 1024 * 1024
LOG2E = math.log2(math.e)
ATTN_CHUNK = 256


def _params(*sem):
    return pltpu.CompilerParams(dimension_semantics=sem, vmem_limit_bytes=VMEM_LIMIT)


def _silu(v):
    return v * jax.nn.sigmoid(v)


_NT_DIMS = (((1,), (1,)), ((), ()))


def _dot_nt(a, b):
    return lax.dot_general(a, b, _NT_DIMS, preferred_element_type=F32)


def _mod_kernel(c_ref, w_ref, b_ref, o_ref):
    s = _silu(c_ref[...]).astype(BF16)
    o_ref[0] = jnp.dot(s, w_ref[0].astype(BF16), preferred_element_type=F32) + b_ref[0]


def _mod_vectors(cvec, w_mod, b_mod):
    depth, d, n3 = w_mod.shape
    tn = 768
    return pl.pallas_call(
        _mod_kernel,
        grid=(depth, n3 // tn),
        in_specs=[
            pl.BlockSpec((MOD_ROWS, d), lambda l, j: (0, 0)),
            pl.BlockSpec((1, d, tn), lambda l, j: (l, 0, j)),
            pl.BlockSpec((1, 1, tn), lambda l, j: (l, 0, j)),
        ],
        out_specs=pl.BlockSpec((1, MOD_ROWS, tn), lambda l, j: (l, 0, j)),
        out_shape=jax.ShapeDtypeStruct((depth, MOD_ROWS, n3), F32),
        compiler_params=_params("arbitrary", "arbitrary"),
        name="mod_vectors",
    )(cvec, w_mod, b_mod.reshape(depth, 1, n3))


def _rms_rope(y, g, cos, sin):
    ms = jnp.mean(y * y, axis=-1, keepdims=True)
    yn = y * lax.rsqrt(ms + EPS) * g
    return yn * cos + pltpu.roll(yn, LANES // 2, 1) * sin


def _modulate(x_ref, sh_ref, sc_ref, xin_ref):
    @pl.when(pl.program_id(1) == 0)
    def _():
        xin_ref[...] = (x_ref[...] * (1.0 + sc_ref[0]) + sh_ref[0]).astype(BF16)


def _plain_kernel(x_ref, sh_ref, sc_ref, w_ref, o_ref, xin_ref, *, act):
    _modulate(x_ref, sh_ref, sc_ref, xin_ref)
    acc = jnp.dot(xin_ref[...], w_ref[...], preferred_element_type=F32)
    o_ref[...] = (_silu(acc) if act == "silu" else acc).astype(o_ref.dtype)


def _q_heads_kernel(x_ref, sh_ref, sc_ref, wt_ref, g_ref, gp_ref, cos_ref, sin_ref, qt_ref, xin_ref, *, q_scale):
    _modulate(x_ref, sh_ref, sc_ref, xin_ref)
    acc = _dot_nt(wt_ref[...], xin_ref[...])
    a = g_ref[...] * cos_ref[...]
    b = gp_ref[...] * sin_ref[...]
    half = LANES // 2
    for h in range(acc.shape[0] // LANES):
        y = acc[h * LANES:(h + 1) * LANES]
        yn = y * (lax.rsqrt(jnp.mean(y * y, axis=0, keepdims=True) + EPS) * q_scale)
        partner = jnp.concatenate([yn[half:], yn[:half]], axis=0)
        qt_ref[h * LANES:(h + 1) * LANES, :] = (yn * a + partner * b).astype(qt_ref.dtype)


def _kv_heads_kernel(x_ref, sh_ref, sc_ref, wk_ref, wvt_ref, g_ref, cos_ref, sin_ref, k_ref, vt_ref, xin_ref):
    _modulate(x_ref, sh_ref, sc_ref, xin_ref)
    xin = xin_ref[...]
    g, cos, sin = g_ref[...], cos_ref[...], sin_ref[...]
    yk = jnp.dot(xin, wk_ref[...], preferred_element_type=F32)
    for h in range(yk.shape[1] // LANES):
        k_ref[:, h * LANES:(h + 1) * LANES] = _rms_rope(
            yk[:, h * LANES:(h + 1) * LANES], g, cos, sin).astype(k_ref.dtype)
    vt_ref[...] = _dot_nt(wvt_ref[...], xin).astype(vt_ref.dtype)


def _mod_specs(d, tm, rows_per_mod, mod_row0):
    if rows_per_mod is None:
        mod_idx = lambda i: mod_row0
    else:
        assert rows_per_mod % tm == 0
        tiles_per_mod = rows_per_mod // tm
        mod_idx = lambda i: mod_row0 + i // tiles_per_mod
    return [
        pl.BlockSpec((tm, d), lambda i, j: (i, 0)),
        pl.BlockSpec((1, 1, d), lambda i, j: (mod_idx(i), 0, 0)),
        pl.BlockSpec((1, 1, d), lambda i, j: (mod_idx(i), 0, 1)),
    ]


def _mod_matmul(x, mod3, w, *, rows_per_mod, mod_row0, tm, tn, act="none"):
    m, d = x.shape
    n = w.shape[1]
    assert m % tm == 0 and n % tn == 0
    return pl.pallas_call(
        functools.partial(_plain_kernel, act=act),
        grid=(m // tm, n // tn),
        in_specs=_mod_specs(d, tm, rows_per_mod, mod_row0) + [pl.BlockSpec((d, tn), lambda i, j: (0, j))],
        out_specs=pl.BlockSpec((tm, tn), lambda i, j: (i, j)),
        out_shape=jax.ShapeDtypeStruct((m, n), BF16),
        scratch_shapes=[pltpu.VMEM((tm, d), BF16)],
        compiler_params=_params("arbitrary", "arbitrary"),
        name="mod_matmul_" + act,
    )(x, mod3, mod3, w)


def _q_heads(x, mod3, wt, gain, gain_partner, cos_t, sin_t, *, rows_per_mod, mod_row0, tm, tn, q_scale):
    m, d = x.shape
    n = wt.shape[0]
    pos = cos_t.shape[1]
    assert m % tm == 0 and n % tn == 0 and pos % tm == 0
    tiles_per_seq = pos // tm
    col = pl.BlockSpec((LANES, 1), lambda i, j: (0, 0))
    tab = pl.BlockSpec((LANES, tm), lambda i, j: (0, i % tiles_per_seq))
    return pl.pallas_call(
        functools.partial(_q_heads_kernel, q_scale=q_scale),
        grid=(m // tm, n // tn),
        in_specs=_mod_specs(d, tm, rows_per_mod, mod_row0) + [
            pl.BlockSpec((tn, d), lambda i, j: (j, 0)), col, col, tab, tab],
        out_specs=pl.BlockSpec((tn, tm), lambda i, j: (j, i)),
        out_shape=jax.ShapeDtypeStruct((n, m), BF16),
        scratch_shapes=[pltpu.VMEM((tm, d), BF16)],
        compiler_params=_params("arbitrary", "arbitrary"),
        name="q_heads",
    )(x, mod3, mod3, wt, gain, gain_partner, cos_t, sin_t)


def _kv_heads(x, mod3, wk, wvt, gain, cos, sin, *, rows_per_mod, mod_row0, tm):
    m, d = x.shape
    nk = wk.shape[1]
    nv = wvt.shape[0]
    pos = cos.shape[0]
    assert m % tm == 0 and pos % tm == 0
    tiles_per_seq = pos // tm
    tab = pl.BlockSpec((tm, LANES), lambda i, j: (i % tiles_per_seq, 0))
    return pl.pallas_call(
        _kv_heads_kernel,
        grid=(m // tm, 1),
        in_specs=_mod_specs(d, tm, rows_per_mod, mod_row0) + [
            pl.BlockSpec((d, nk), lambda i, j: (0, 0)),
            pl.BlockSpec((nv, d), lambda i, j: (0, 0)),
            pl.BlockSpec((1, LANES), lambda i, j: (0, 0)), tab, tab],
        out_specs=[pl.BlockSpec((tm, nk), lambda i, j: (i, 0)), pl.BlockSpec((nv, tm), lambda i, j: (0, i))],
        out_shape=[jax.ShapeDtypeStruct((m, nk), BF16), jax.ShapeDtypeStruct((nv, m), BF16)],
        scratch_shapes=[pltpu.VMEM((tm, d), BF16)],
        compiler_params=_params("arbitrary", "arbitrary"),
        name="kv_heads",
    )(x, mod3, mod3, wk, wvt, gain, cos, sin)


def _mla_proj_kernel(ql_ref, kvl_ref, kr_ref, qg_ref, kvg_ref, wqt_ref, wk_ref, wvt_ref, cs_ref, cst_ref,
                     qt_ref, k_ref, vt_ref, *, q_scale):
    def rms(v, g):
        vf = v.astype(F32)
        return (vf * lax.rsqrt(jnp.mean(vf * vf, axis=-1, keepdims=True) + EPS) * g).astype(BF16)

    yqt = _dot_nt(wqt_ref[...], rms(ql_ref[...], qg_ref[...]))
    cst = cst_ref[...]
    for h in range(MLA_HEADS):
        lo = h * MLA_QK_PAD
        qt_ref[lo:lo + MLA_NOPE, :] = (yqt[lo:lo + MLA_NOPE] * q_scale).astype(qt_ref.dtype)
        t = yqt[lo + MLA_NOPE:lo + MLA_QK_PAD] * cst
        rope = (t[:MLA_ROPE] + t[MLA_ROPE:]) * q_scale
        qt_ref[lo + MLA_NOPE:lo + MLA_NOPE + MLA_ROPE, :] = rope.astype(qt_ref.dtype)
        qt_ref[lo + MLA_NOPE + MLA_ROPE:lo + MLA_QK_PAD, :] = jnp.zeros((MLA_ROPE, rope.shape[1]), qt_ref.dtype)

    rkv = rms(kvl_ref[...], kvg_ref[...])
    yk = jnp.dot(rkv, wk_ref[...], preferred_element_type=F32)
    t = kr_ref[...].astype(F32) * cs_ref[...]
    kr = t + pltpu.roll(t, LANES // 2, 1)
    lane = lax.broadcasted_iota(jnp.int32, kr.shape, 1)
    kr = jnp.where(lane < MLA_ROPE, kr, 0.0).astype(k_ref.dtype)
    for h in range(MLA_HEADS):
        lo = h * MLA_QK_PAD
        k_ref[:, lo:lo + MLA_NOPE] = yk[:, h * MLA_NOPE:(h + 1) * MLA_NOPE].astype(k_ref.dtype)
        k_ref[:, lo + MLA_NOPE:lo + MLA_QK_PAD] = kr
    vt_ref[...] = _dot_nt(wvt_ref[...], rkv).astype(vt_ref.dtype)


def _mla_proj(p_lat, q_gain, kv_gain, wqt, wk, wvt, cs_tab, cs_tab_t, *, tm, q_scale):
    m = p_lat.shape[0]
    pos_rows = cs_tab.shape[0]
    assert m % tm == 0 and pos_rows % tm == 0
    tiles_per_seq = pos_rows // tm
    hq = MLA_HEADS * MLA_QK_PAD
    hv = MLA_HEADS * MLA_V
    rope_blk = (MLA_Q_RANK + MLA_KV_RANK) // LANES
    return pl.pallas_call(
        functools.partial(_mla_proj_kernel, q_scale=q_scale),
        grid=(m // tm,),
        in_specs=[
            pl.BlockSpec((tm, MLA_Q_RANK), lambda i: (i, 0)),
            pl.BlockSpec((tm, MLA_KV_RANK), lambda i: (i, 1)),
            pl.BlockSpec((tm, LANES), lambda i: (i, rope_blk)),
            pl.BlockSpec((1, MLA_Q_RANK), lambda i: (0, 0)),
            pl.BlockSpec((1, MLA_KV_RANK), lambda i: (0, 0)),
            pl.BlockSpec((hq, MLA_Q_RANK), lambda i: (0, 0)),
            pl.BlockSpec((MLA_KV_RANK, hv), lambda i: (0, 0)),
            pl.BlockSpec((hv, MLA_KV_RANK), lambda i: (0, 0)),
            pl.BlockSpec((tm, LANES), lambda i: (i % tiles_per_seq, 0)),
            pl.BlockSpec((LANES, tm), lambda i: (0, i % tiles_per_seq)),
        ],
        out_specs=[
            pl.BlockSpec((hq, tm), lambda i: (0, i)),
            pl.BlockSpec((tm, hq), lambda i: (i, 0)),
            pl.BlockSpec((hv, tm), lambda i: (0, i)),
        ],
        out_shape=[
            jax.ShapeDtypeStruct((hq, m), BF16),
            jax.ShapeDtypeStruct((m, hq), BF16),
            jax.ShapeDtypeStruct((hv, m), BF16),
        ],
        compiler_params=_params("arbitrary"),
        name="mla_proj",
    )(p_lat, p_lat, p_lat, q_gain, kv_gain, wqt, wk, wvt, cs_tab, cs_tab_t)


def _attn_kernel(*refs, sub, heads, group, dk, lat_keys, ctx_keys):
    if lat_keys:
        qt_ref, kl_ref, vtl_ref, kc_ref, vtc_ref, o_ref, s_a, s_b = refs
    else:
        qt_ref, kc_ref, vtc_ref, o_ref, s_a, s_b = refs
        kl_ref = vtl_ref = None
    n_sub = qt_ref.shape[1] // sub
    n_elems = heads * n_sub
    pieces = [(kl_ref, vtl_ref, c * ATTN_CHUNK, c * ATTN_CHUNK, ATTN_CHUNK) for c in range(lat_keys // ATTN_CHUNK)]
    pieces.append((kc_ref, vtc_ref, 0, lat_keys, ctx_keys))

    def elem(t):
        if isinstance(t, int):
            h, j = t // n_sub, t % n_sub
            return h, h // group, j, (s_a, s_b)[t % 2]
        assert n_sub & (n_sub - 1) == 0 and group & (group - 1) == 0
        h = lax.shift_right_logical(t, n_sub.bit_length() - 1)
        return h, lax.shift_right_logical(h, group.bit_length() - 1), lax.bitwise_and(t, n_sub - 1), None

    def fold8(v):
        return v.reshape(v.shape[0] // 8, 8, v.shape[1])

    def scores(slot, qt, k_rows, row0, rows):
        s = jnp.dot(k_rows, qt, preferred_element_type=F32)
        slot[row0:row0 + rows, :] = s
        return jnp.max(fold8(s), axis=0)

    def weighted(slot, m, vt_cols, row0, rows):
        p = jnp.exp2(slot[row0:row0 + rows, :] - m)
        return jnp.dot(vt_cols, p.astype(BF16), preferred_element_type=F32), jnp.sum(fold8(p), axis=0)

    def stage(cur, prev, m2):
        mrun = jnp.full((8, sub), -jnp.inf, F32)
        lrun = jnp.zeros((8, sub), F32)
        acc = None
        if cur is not None:
            h1, hk1, j1, slot1 = cur
            qt = qt_ref[pl.ds(h1 * dk, dk), pl.ds(j1 * sub, sub)]
        if prev is not None:
            h2, hk2, j2, slot2 = prev
        for k_ref, vt_ref, src0, dst0, rows in pieces:
            if prev is not None:
                d, lsum = weighted(slot2, m2, vt_ref[pl.ds(hk2 * HEAD_V, HEAD_V), src0:src0 + rows], dst0, rows)
                acc = d if acc is None else acc + d
                lrun = lrun + lsum
            if cur is not None:
                cm = scores(slot1, qt, k_ref[src0:src0 + rows, pl.ds(hk1 * dk, dk)], dst0, rows)
                mrun = jnp.maximum(mrun, cm)
        if prev is not None:
            ot = acc / jnp.sum(lrun, axis=0, keepdims=True)
            o_ref[pl.ds(j2 * sub, sub), pl.ds(h2 * HEAD_V, HEAD_V)] = ot.T.astype(o_ref.dtype)
        return jnp.max(mrun, axis=0, keepdims=True) if cur is not None else None

    m = stage(elem(0), None, None)

    def stage_pair(i, m):
        t = 2 * i + 1
        h0, hk0, j0, _ = elem(t - 1)
        h1, hk1, j1, _ = elem(t)
        h2, hk2, j2, _ = elem(t + 1)
        m = stage((h1, hk1, j1, s_b), (h0, hk0, j0, s_a), m)
        return stage((h2, hk2, j2, s_a), (h1, hk1, j1, s_b), m)

    n_pairs = (n_elems - 1) // 2
    if n_pairs:
        m = lax.fori_loop(0, n_pairs, stage_pair, m)
    for t in range(2 * n_pairs + 1, n_elems):
        m = stage(elem(t), elem(t - 1), m)
    stage(None, elem(n_elems - 1), m)


def _attention(qt, kv_lat, kv_ctx, *, batch, heads, heads_per_step, group, dk, sub, lat_keys, ctx_keys):
    mq = qt.shape[1]
    nq = mq // batch
    hs = heads_per_step
    assert nq % sub == 0 and heads % hs == 0 and (hs % group == 0 or group % hs == 0)
    kvs = max(1, hs // group)
    kv_blk = lambda g: g * hs // (group * kvs)
    in_specs = [pl.BlockSpec((hs * dk, nq), lambda b, g: (g, b))]
    operands = [qt]
    n_keys = ctx_keys
    if kv_lat is not None:
        assert lat_keys % ATTN_CHUNK == 0
        n_keys += lat_keys
        in_specs += [
            pl.BlockSpec((lat_keys, kvs * dk), lambda b, g: (b, kv_blk(g))),
            pl.BlockSpec((kvs * HEAD_V, lat_keys), lambda b, g: (kv_blk(g), b)),
        ]
        operands += list(kv_lat)
    in_specs += [
        pl.BlockSpec((ctx_keys, kvs * dk), lambda b, g: (b, kv_blk(g))),
        pl.BlockSpec((kvs * HEAD_V, ctx_keys), lambda b, g: (kv_blk(g), b)),
    ]
    operands += list(kv_ctx)
    return pl.pallas_call(
        functools.partial(_attn_kernel, sub=sub, heads=hs, group=min(group, hs), dk=dk,
                          lat_keys=lat_keys if kv_lat is not None else 0, ctx_keys=ctx_keys),
        grid=(batch, heads // hs),
        in_specs=in_specs,
        out_specs=pl.BlockSpec((nq, hs * HEAD_V), lambda b, g: (b, g)),
        out_shape=jax.ShapeDtypeStruct((mq, heads * HEAD_V), BF16),
        scratch_shapes=[pltpu.VMEM((n_keys, sub), F32), pltpu.VMEM((n_keys, sub), F32)],
        compiler_params=_params("arbitrary", "arbitrary"),
        name="attention",
    )(*operands)


def _conv_kernel(cb_ref, cc_ref, cu_ref, w_ref, o_ref):
    t = cc_ref[...].astype(F32) * cu_ref[...].astype(F32)
    n = t.shape[0]
    row = lax.broadcasted_iota(jnp.int32, t.shape, 0)
    t_prev = jnp.where(row == 0, 0.0, pltpu.roll(t, 1, 0))
    t_next = jnp.where(row == n - 1, 0.0, pltpu.roll(t, n - 1, 0))
    w = w_ref[...]
    conv = t_prev * w[0:1, :] + t * w[1:2, :] + t_next * w[2:3, :]
    o_ref[...] = (cb_ref[...].astype(F32) * conv).astype(o_ref.dtype)


def _short_conv_gate(p_conv, conv_w, *, batch, width):
    m = p_conv.shape[0]
    n = m // batch
    cblk = CONV_WIDTH // width
    return pl.pallas_call(
        _conv_kernel,
        grid=(batch, cblk),
        in_specs=[
            pl.BlockSpec((n, width), lambda b, j: (b, j)),
            pl.BlockSpec((n, width), lambda b, j: (b, cblk + j)),
            pl.BlockSpec((n, width), lambda b, j: (b, 2 * cblk + j)),
            pl.BlockSpec((3, width), lambda b, j: (0, j)),
        ],
        out_specs=pl.BlockSpec((n, width), lambda b, j: (b, j)),
        out_shape=jax.ShapeDtypeStruct((m, CONV_WIDTH), BF16),
        compiler_params=_params("arbitrary", "arbitrary"),
        name="short_conv",
    )(p_conv, p_conv, p_conv, conv_w)


def _out_kernel(*refs, n_parts):
    parts = refs[:n_parts]
    g_ref, w_ref, x_ref, gate_ref, lng_ref, lnb_ref, o_ref = refs[n_parts:]
    half = x_ref.shape[0] // 2
    ys = []
    for r0 in (0, half):
        y = None
        off = 0
        for p_ref in parts:
            kp = p_ref.shape[1]
            u = p_ref[r0:r0 + half, :] * g_ref[r0:r0 + half, off:off + kp]
            part = jnp.dot(u, w_ref[off:off + kp, :], preferred_element_type=F32)
            y = part if y is None else y + part
            off += kp
        ys.append(y)
    for r0, y in zip((0, half), ys):
        z = ALPHA * x_ref[r0:r0 + half, :] + gate_ref[0] * y
        mu = jnp.mean(z, axis=-1, keepdims=True)
        zc = z - mu
        var = jnp.mean(zc * zc, axis=-1, keepdims=True)
        o_ref[r0:r0 + half, :] = zc * lax.rsqrt(var + EPS) * lng_ref[...] + lnb_ref[...]


def _out_proj_norm(parts, g, w, x, mod3, ln_g, ln_b, *, rows_per_mod, mod_row0, tm):
    m, d = x.shape
    assert m % tm == 0
    if rows_per_mod is None:
        mod_idx = lambda i: mod_row0
    else:
        tiles_per_mod = rows_per_mod // tm
        mod_idx = lambda i: mod_row0 + i // tiles_per_mod
    in_specs = [pl.BlockSpec((tm, p.shape[1]), lambda i: (i, 0)) for p in parts]
    in_specs += [
        pl.BlockSpec((tm, g.shape[1]), lambda i: (i, 0)),
        pl.BlockSpec(w.shape, lambda i: (0, 0), pipeline_mode=pl.Buffered(1)),
        pl.BlockSpec((tm, d), lambda i: (i, 0)),
        pl.BlockSpec((1, 1, d), lambda i: (mod_idx(i), 0, 2)),
        pl.BlockSpec((1, d), lambda i: (0, 0)),
        pl.BlockSpec((1, d), lambda i: (0, 0)),
    ]
    return pl.pallas_call(
        functools.partial(_out_kernel, n_parts=len(parts)),
        grid=(m // tm,),
        in_specs=in_specs,
        out_specs=pl.BlockSpec((tm, d), lambda i: (i, 0)),
        out_shape=jax.ShapeDtypeStruct((m, d), F32),
        compiler_params=_params("arbitrary"),
        name="out_proj_norm",
    )(*parts, g, w, x, mod3, ln_g, ln_b)


def _rope_tables(n, ctx_len):
    rows = n // GRID_W
    pos_row = np.repeat(np.arange(rows, dtype=np.float32), GRID_W)
    pos_col = np.tile(np.arange(GRID_W, dtype=np.float32), rows)

    def angles(pos, half):
        inv = np.float32(ROPE_THETA) ** (-np.arange(half, dtype=np.float32) / np.float32(half))
        return (pos[:, None] * inv[None, :]).astype(np.float32)

    f32 = np.float32
    ar, ac = angles(pos_row, MLA_ROPE // 4), angles(pos_col, MLA_ROPE // 4)
    cs_mla = np.concatenate([np.cos(ar), np.cos(ar), np.cos(ac), np.cos(ac),
                             -np.sin(ar), np.sin(ar), -np.sin(ac), np.sin(ac)], axis=1).astype(f32)
    cs_mla_ctx = np.concatenate([np.ones((ctx_len, MLA_ROPE), f32), np.zeros((ctx_len, MLA_ROPE), f32)], axis=1)
    br, bc = angles(pos_row, GQA_HEAD_DIM // 4), angles(pos_col, GQA_HEAD_DIM // 4)
    cos_gqa = np.concatenate([np.cos(br), np.cos(bc), np.cos(br), np.cos(bc)], axis=1).astype(f32)
    sin_gqa = np.concatenate([-np.sin(br), -np.sin(bc), np.sin(br), np.sin(bc)], axis=1).astype(f32)
    cos_ctx = np.ones((ctx_len, GQA_HEAD_DIM), f32)
    sin_ctx = np.zeros((ctx_len, GQA_HEAD_DIM), f32)
    tabs = dict(mla=cs_mla, mla_ctx=cs_mla_ctx, mla_t=cs_mla.T, mla_ctx_t=cs_mla_ctx.T,
                cos=cos_gqa, sin=sin_gqa, cos_t=cos_gqa.T, sin_t=sin_gqa.T, cos_ctx=cos_ctx, sin_ctx=sin_ctx)
    return {k: jnp.asarray(np.ascontiguousarray(v)) for k, v in tabs.items()}


def kernel(x, c, ctx, c_ctx, w_mod, b_mod, ln_g, ln_b, a_w_in, a_conv_w, a_q_norm, a_w_qb, a_kv_norm,
           a_w_kvb, a_w_out, c_w_in, c_q_norm, c_k_norm, c_w_out):
    batch, n, d = x.shape
    ctx_len = ctx.shape[1]
    assert d == D_MODEL and batch + 1 <= MOD_ROWS
    ctx_mod_row = batch

    tabs = _rope_tables(n, ctx_len)

    cvec = jnp.concatenate([c, c_ctx[None, :], jnp.zeros((MOD_ROWS - batch - 1, d), F32)], axis=0)
    mods = _mod_vectors(cvec, w_mod, b_mod)
    mod0 = mods[0].reshape(MOD_ROWS, 1, 3 * d)
    mod1 = mods[1].reshape(MOD_ROWS, 1, 3 * d)

    xl = x.reshape(batch * n, d)
    xc = ctx.reshape(batch * ctx_len, d)

    w_in0 = a_w_in[0]
    o_c = 3 * CONV_WIDTH
    o_q = o_c + MLA_Q_RANK
    o_kv = o_q + MLA_KV_RANK
    o_g = o_kv + MLA_ROPE
    swap16 = np.concatenate([np.arange(16, 32), np.arange(0, 16), np.arange(48, 64), np.arange(32, 48)])
    w_conv = w_in0[:, :o_c].astype(BF16)
    w_rope = w_in0[:, o_kv:o_g]
    w_lat = jnp.concatenate([w_in0[:, o_c:o_kv], w_rope, w_rope[:, swap16]], axis=1).astype(BF16)
    w_gate0 = w_in0[:, o_g:].astype(BF16)
    wq3 = a_w_qb[0].reshape(MLA_Q_RANK, MLA_HEADS, MLA_NOPE + MLA_ROPE)
    wq_rope = wq3[:, :, MLA_NOPE:]
    wq = jnp.concatenate([wq3[:, :, :MLA_NOPE], wq_rope, wq_rope[:, :, swap16]], axis=2)
    wq_t = wq.reshape(MLA_Q_RANK, MLA_HEADS * MLA_QK_PAD).T.astype(BF16)
    wkv3 = a_w_kvb[0].reshape(MLA_KV_RANK, MLA_HEADS, MLA_NOPE + MLA_V)
    wk0 = wkv3[:, :, :MLA_NOPE].reshape(MLA_KV_RANK, -1).astype(BF16)
    wv0_t = wkv3[:, :, MLA_NOPE:].reshape(MLA_KV_RANK, -1).T.astype(BF16)
    w_out0 = a_w_out[0].astype(BF16)

    w_in1 = c_w_in[0]
    nq1 = GQA_HEADS * GQA_HEAD_DIM
    nk1 = GQA_KV_HEADS * GQA_HEAD_DIM
    perm = np.concatenate([np.arange(0, 32), np.arange(64, 96), np.arange(32, 64), np.arange(96, 128)])
    partner = np.concatenate([np.arange(64, 128), np.arange(0, 64)])
    w_q1_t = w_in1[:, :nq1].reshape(d, GQA_HEADS, GQA_HEAD_DIM)[:, :, perm].reshape(d, nq1).T.astype(BF16)
    w_k1 = w_in1[:, nq1:nq1 + nk1].reshape(d, GQA_KV_HEADS, GQA_HEAD_DIM)[:, :, perm].reshape(d, nk1).astype(BF16)
    w_v1_t = w_in1[:, nq1 + nk1:nq1 + 2 * nk1].T.astype(BF16)
    w_gate1 = w_in1[:, nq1 + 2 * nk1:].astype(BF16)
    q_gain1 = c_q_norm[0][perm].reshape(GQA_HEAD_DIM, 1)
    q_gain1_partner = c_q_norm[0][perm][partner].reshape(GQA_HEAD_DIM, 1)
    k_gain1 = c_k_norm[0][perm].reshape(1, GQA_HEAD_DIM)
    w_out1 = c_w_out[0].astype(BF16)

    lng0, lnb0 = ln_g[0].reshape(1, d), ln_b[0].reshape(1, d)
    lng1, lnb1 = ln_g[1].reshape(1, d), ln_b[1].reshape(1, d)

    lat = dict(rows_per_mod=n, mod_row0=0)
    cxt = dict(rows_per_mod=None, mod_row0=ctx_mod_row)
    tm_l, tm_c = 1024, ctx_len
    q_scale0 = (MLA_NOPE + MLA_ROPE) ** -0.5 * LOG2E
    q_scale1 = GQA_HEAD_DIM ** -0.5 * LOG2E

    def layer0_front(rows, mod_kw, tm_proj, cs_tab, cs_tab_t):
        p_conv = _mod_matmul(rows, mod0, w_conv, tm=tm_l, tn=1024, **mod_kw)
        p_lat = _mod_matmul(rows, mod0, w_lat, tm=tm_l, tn=w_lat.shape[1], **mod_kw)
        gate = _mod_matmul(rows, mod0, w_gate0, tm=tm_l, tn=1024, act="silu", **mod_kw)
        qt, k, vt = _mla_proj(p_lat, a_q_norm[0].reshape(1, -1), a_kv_norm[0].reshape(1, -1), wq_t, wk0, wv0_t,
                              cs_tab, cs_tab_t, tm=tm_proj, q_scale=q_scale0)
        return p_conv, gate, qt, k, vt

    pcl, gl, qtl, kl, vtl = layer0_front(xl, lat, 512, tabs["mla"], tabs["mla_t"])
    pcc, gc, qtc, kc, vtc = layer0_front(xc, cxt, tm_c, tabs["mla_ctx"], tabs["mla_ctx_t"])

    mla = dict(batch=batch, heads=MLA_HEADS, group=1, dk=MLA_QK_PAD, lat_keys=n, ctx_keys=ctx_len)
    attn_l = _attention(qtl, (kl, vtl), (kc, vtc), heads_per_step=2, sub=512, **mla)
    attn_c = _attention(qtc, None, (kc, vtc), heads_per_step=MLA_HEADS, sub=ctx_len, **mla)

    conv_l = _short_conv_gate(pcl, a_conv_w[0], batch=batch, width=LANES)
    conv_c = _short_conv_gate(pcc, a_conv_w[0], batch=batch, width=CONV_WIDTH)

    x1 = _out_proj_norm([conv_l, attn_l], gl, w_out0, xl, mod0, lng0, lnb0, tm=512, **lat)
    h1 = _out_proj_norm([conv_c, attn_c], gc, w_out0, xc, mod0, lng0, lnb0, tm=256, **cxt)

    qt1 = _q_heads(x1, mod1, w_q1_t, q_gain1, q_gain1_partner, tabs["cos_t"], tabs["sin_t"],
                   tm=tm_l, tn=1024, q_scale=q_scale1, **lat)
    k1, vt1 = _kv_heads(x1, mod1, w_k1, w_v1_t, k_gain1, tabs["cos"], tabs["sin"], tm=tm_l, **lat)
    g1 = _mod_matmul(x1, mod1, w_gate1, tm=tm_l, tn=1024, act="silu", **lat)
    kc1, vtc1 = _kv_heads(h1, mod1, w_k1, w_v1_t, k_gain1, tabs["cos_ctx"], tabs["sin_ctx"], tm=tm_c, **cxt)

    attn1 = _attention(qt1, (k1, vt1), (kc1, vtc1), batch=batch, heads=GQA_HEADS, heads_per_step=GQA_GROUP,
                       group=GQA_GROUP, dk=GQA_HEAD_DIM, sub=512, lat_keys=n, ctx_keys=ctx_len)

    out = _out_proj_norm([attn1], g1, w_out1, x1, mod1, lng1, lnb1, tm=512, **lat)
    return out.reshape(batch, n, d)
```

```python
import functools
import math

import jax
import jax.numpy as jnp
import numpy as np
from jax import lax
from jax.experimental import pallas as pl
from jax.experimental.pallas import tpu as pltpu

F32 = jnp.float32
BF16 = jnp.bfloat16

D_MODEL = 2048
DEPTH = 2
GRID_W = 64
ROPE_THETA = 10000.0
EPS = 1e-6
ALPHA = (2 * DEPTH) ** 0.25

CONV_WIDTH = D_MODEL // 2
MLA_HEADS = D_MODEL // 256
MLA_Q_RANK = 512
MLA_KV_RANK = 512
MLA_NOPE = 128
MLA_ROPE = 64
MLA_V = 128
MLA_QK_PAD = 256
GQA_HEAD_DIM = 128
GQA_HEADS = D_MODEL // GQA_HEAD_DIM
GQA_KV_HEADS = GQA_HEADS // 4
GQA_GROUP = GQA_HEADS // GQA_KV_HEADS

LANES = 128
HEAD_V = 128
MOD_ROWS = 16
VMEM_LIMIT = 56 * 1024 * 1024
LOG2E = math.log2(math.e)
ATTN_CHUNK = 256
ATTN_SKEW = 1


def _params(*sem):
    return pltpu.CompilerParams(dimension_semantics=sem, vmem_limit_bytes=VMEM_LIMIT)


def _silu(v):
    return v * jax.nn.sigmoid(v)


_NT_DIMS = (((1,), (1,)), ((), ()))


def _dot_nt(a, b):
    return lax.dot_general(a, b, _NT_DIMS, preferred_element_type=F32)


def _mod_kernel(c_ref, w_ref, b_ref, o_ref):
    s = _silu(c_ref[...]).astype(BF16)
    o_ref[0] = jnp.dot(s, w_ref[0].astype(BF16), preferred_element_type=F32) + b_ref[0]


def _mod_vectors(cvec, w_mod, b_mod):
    depth, d, n3 = w_mod.shape
    tn = 768
    return pl.pallas_call(
        _mod_kernel,
        grid=(depth, n3 // tn),
        in_specs=[
            pl.BlockSpec((MOD_ROWS, d), lambda l, j: (0, 0)),
            pl.BlockSpec((1, d, tn), lambda l, j: (l, 0, j)),
            pl.BlockSpec((1, 1, tn), lambda l, j: (l, 0, j)),
        ],
        out_specs=pl.BlockSpec((1, MOD_ROWS, tn), lambda l, j: (l, 0, j)),
        out_shape=jax.ShapeDtypeStruct((depth, MOD_ROWS, n3), F32),
        compiler_params=_params("arbitrary", "arbitrary"),
        name="mod_vectors",
    )(cvec, w_mod, b_mod.reshape(depth, 1, n3))


def _rms_rope(y, g, cos, sin):
    ms = jnp.mean(y * y, axis=-1, keepdims=True)
    yn = y * lax.rsqrt(ms + EPS) * g
    return yn * cos + pltpu.roll(yn, LANES // 2, 1) * sin


def _modulate(x_ref, sh_ref, sc_ref, xin_ref):
    @pl.when(pl.program_id(1) == 0)
    def _():
        xin_ref[...] = (x_ref[...] * (1.0 + sc_ref[0]) + sh_ref[0]).astype(BF16)


def _plain_kernel(x_ref, sh_ref, sc_ref, w_ref, o_ref, xin_ref, *, act):
    _modulate(x_ref, sh_ref, sc_ref, xin_ref)
    half = xin_ref.shape[0] // 2
    w = w_ref[...]
    accs = [jnp.dot(xin_ref[r0:r0 + half, :], w, preferred_element_type=F32) for r0 in (0, half)]
    for r0, acc in zip((0, half), accs):
        o_ref[r0:r0 + half, :] = (_silu(acc) if act == "silu" else acc).astype(o_ref.dtype)


def _q_heads_kernel(x_ref, sh_ref, sc_ref, wt_ref, g_ref, gp_ref, cos_ref, sin_ref, qt_ref, xin_ref, *, q_scale):
    _modulate(x_ref, sh_ref, sc_ref, xin_ref)
    acc = _dot_nt(wt_ref[...], xin_ref[...])
    a = g_ref[...] * cos_ref[...]
    b = gp_ref[...] * sin_ref[...]
    half = LANES // 2
    for h in range(acc.shape[0] // LANES):
        y = acc[h * LANES:(h + 1) * LANES]
        yn = y * (lax.rsqrt(jnp.mean(y * y, axis=0, keepdims=True) + EPS) * q_scale)
        partner = jnp.concatenate([yn[half:], yn[:half]], axis=0)
        qt_ref[h * LANES:(h + 1) * LANES, :] = (yn * a + partner * b).astype(qt_ref.dtype)


def _kv_heads_kernel(x_ref, sh_ref, sc_ref, wk_ref, wvt_ref, g_ref, cos_ref, sin_ref, k_ref, vt_ref, xin_ref):
    _modulate(x_ref, sh_ref, sc_ref, xin_ref)
    xin = xin_ref[...]
    g, cos, sin = g_ref[...], cos_ref[...], sin_ref[...]
    yk = jnp.dot(xin, wk_ref[...], preferred_element_type=F32)
    for h in range(yk.shape[1] // LANES):
        k_ref[:, h * LANES:(h + 1) * LANES] = _rms_rope(
            yk[:, h * LANES:(h + 1) * LANES], g, cos, sin).astype(k_ref.dtype)
    vt_ref[...] = _dot_nt(wvt_ref[...], xin).astype(vt_ref.dtype)


def _mod_specs(d, tm, rows_per_mod, mod_row0):
    if rows_per_mod is None:
        mod_idx = lambda i: mod_row0
    else:
        assert rows_per_mod % tm == 0
        tiles_per_mod = rows_per_mod // tm
        mod_idx = lambda i: mod_row0 + i // tiles_per_mod
    return [
        pl.BlockSpec((tm, d), lambda i, j: (i, 0)),
        pl.BlockSpec((1, 1, d), lambda i, j: (mod_idx(i), 0, 0)),
        pl.BlockSpec((1, 1, d), lambda i, j: (mod_idx(i), 0, 1)),
    ]


def _mod_matmul(x, mod3, w, *, rows_per_mod, mod_row0, tm, tn, act="none"):
    m, d = x.shape
    n = w.shape[1]
    assert m % tm == 0 and n % tn == 0
    return pl.pallas_call(
        functools.partial(_plain_kernel, act=act),
        grid=(m // tm, n // tn),
        in_specs=_mod_specs(d, tm, rows_per_mod, mod_row0) + [pl.BlockSpec((d, tn), lambda i, j: (0, j))],
        out_specs=pl.BlockSpec((tm, tn), lambda i, j: (i, j)),
        out_shape=jax.ShapeDtypeStruct((m, n), BF16),
        scratch_shapes=[pltpu.VMEM((tm, d), BF16)],
        compiler_params=_params("arbitrary", "arbitrary"),
        name="mod_matmul_" + act,
    )(x, mod3, mod3, w)


def _q_heads(x, mod3, wt, gain, gain_partner, cos_t, sin_t, *, rows_per_mod, mod_row0, tm, tn, q_scale):
    m, d = x.shape
    n = wt.shape[0]
    pos = cos_t.shape[1]
    assert m % tm == 0 and n % tn == 0 and pos % tm == 0
    tiles_per_seq = pos // tm
    col = pl.BlockSpec((LANES, 1), lambda i, j: (0, 0))
    tab = pl.BlockSpec((LANES, tm), lambda i, j: (0, i % tiles_per_seq))
    return pl.pallas_call(
        functools.partial(_q_heads_kernel, q_scale=q_scale),
        grid=(m // tm, n // tn),
        in_specs=_mod_specs(d, tm, rows_per_mod, mod_row0) + [
            pl.BlockSpec((tn, d), lambda i, j: (j, 0)), col, col, tab, tab],
        out_specs=pl.BlockSpec((tn, tm), lambda i, j: (j, i)),
        out_shape=jax.ShapeDtypeStruct((n, m), BF16),
        scratch_shapes=[pltpu.VMEM((tm, d), BF16)],
        compiler_params=_params("arbitrary", "arbitrary"),
        name="q_heads",
    )(x, mod3, mod3, wt, gain, gain_partner, cos_t, sin_t)


def _kv_heads(x, mod3, wk, wvt, gain, cos, sin, *, rows_per_mod, mod_row0, tm):
    m, d = x.shape
    nk = wk.shape[1]
    nv = wvt.shape[0]
    pos = cos.shape[0]
    assert m % tm == 0 and pos % tm == 0
    tiles_per_seq = pos // tm
    tab = pl.BlockSpec((tm, LANES), lambda i, j: (i % tiles_per_seq, 0))
    return pl.pallas_call(
        _kv_heads_kernel,
        grid=(m // tm, 1),
        in_specs=_mod_specs(d, tm, rows_per_mod, mod_row0) + [
            pl.BlockSpec((d, nk), lambda i, j: (0, 0)),
            pl.BlockSpec((nv, d), lambda i, j: (0, 0)),
            pl.BlockSpec((1, LANES), lambda i, j: (0, 0)), tab, tab],
        out_specs=[pl.BlockSpec((tm, nk), lambda i, j: (i, 0)), pl.BlockSpec((nv, tm), lambda i, j: (0, i))],
        out_shape=[jax.ShapeDtypeStruct((m, nk), BF16), jax.ShapeDtypeStruct((nv, m), BF16)],
        scratch_shapes=[pltpu.VMEM((tm, d), BF16)],
        compiler_params=_params("arbitrary", "arbitrary"),
        name="kv_heads",
    )(x, mod3, mod3, wk, wvt, gain, cos, sin)


def _mla_proj_kernel(ql_ref, kvl_ref, kr_ref, qg_ref, kvg_ref, wqt_ref, wk_ref, wvt_ref, cs_ref, cst_ref,
                     qt_ref, k_ref, vt_ref, *, q_scale):
    def rms(v, g):
        vf = v.astype(F32)
        return (vf * lax.rsqrt(jnp.mean(vf * vf, axis=-1, keepdims=True) + EPS) * g).astype(BF16)

    yqt = _dot_nt(wqt_ref[...], rms(ql_ref[...], qg_ref[...]))
    cst = cst_ref[...]
    for h in range(MLA_HEADS):
        lo = h * MLA_QK_PAD
        qt_ref[lo:lo + MLA_NOPE, :] = (yqt[lo:lo + MLA_NOPE] * q_scale).astype(qt_ref.dtype)
        t = yqt[lo + MLA_NOPE:lo + MLA_QK_PAD] * cst
        rope = (t[:MLA_ROPE] + t[MLA_ROPE:]) * q_scale
        qt_ref[lo + MLA_NOPE:lo + MLA_NOPE + MLA_ROPE, :] = rope.astype(qt_ref.dtype)
        qt_ref[lo + MLA_NOPE + MLA_ROPE:lo + MLA_QK_PAD, :] = jnp.zeros((MLA_ROPE, rope.shape[1]), qt_ref.dtype)

    rkv = rms(kvl_ref[...], kvg_ref[...])
    yk = jnp.dot(rkv, wk_ref[...], preferred_element_type=F32)
    t = kr_ref[...].astype(F32) * cs_ref[...]
    kr = t + pltpu.roll(t, LANES // 2, 1)
    lane = lax.broadcasted_iota(jnp.int32, kr.shape, 1)
    kr = jnp.where(lane < MLA_ROPE, kr, 0.0).astype(k_ref.dtype)
    for h in range(MLA_HEADS):
        lo = h * MLA_QK_PAD
        k_ref[:, lo:lo + MLA_NOPE] = yk[:, h * MLA_NOPE:(h + 1) * MLA_NOPE].astype(k_ref.dtype)
        k_ref[:, lo + MLA_NOPE:lo + MLA_QK_PAD] = kr
    vt_ref[...] = _dot_nt(wvt_ref[...], rkv).astype(vt_ref.dtype)


def _mla_proj(p_lat, q_gain, kv_gain, wqt, wk, wvt, cs_tab, cs_tab_t, *, tm, q_scale):
    m = p_lat.shape[0]
    pos_rows = cs_tab.shape[0]
    assert m % tm == 0 and pos_rows % tm == 0
    tiles_per_seq = pos_rows // tm
    hq = MLA_HEADS * MLA_QK_PAD
    hv = MLA_HEADS * MLA_V
    rope_blk = (MLA_Q_RANK + MLA_KV_RANK) // LANES
    return pl.pallas_call(
        functools.partial(_mla_proj_kernel, q_scale=q_scale),
        grid=(m // tm,),
        in_specs=[
            pl.BlockSpec((tm, MLA_Q_RANK), lambda i: (i, 0)),
            pl.BlockSpec((tm, MLA_KV_RANK), lambda i: (i, 1)),
            pl.BlockSpec((tm, LANES), lambda i: (i, rope_blk)),
            pl.BlockSpec((1, MLA_Q_RANK), lambda i: (0, 0)),
            pl.BlockSpec((1, MLA_KV_RANK), lambda i: (0, 0)),
            pl.BlockSpec((hq, MLA_Q_RANK), lambda i: (0, 0)),
            pl.BlockSpec((MLA_KV_RANK, hv), lambda i: (0, 0)),
            pl.BlockSpec((hv, MLA_KV_RANK), lambda i: (0, 0)),
            pl.BlockSpec((tm, LANES), lambda i: (i % tiles_per_seq, 0)),
            pl.BlockSpec((LANES, tm), lambda i: (0, i % tiles_per_seq)),
        ],
        out_specs=[
            pl.BlockSpec((hq, tm), lambda i: (0, i)),
            pl.BlockSpec((tm, hq), lambda i: (i, 0)),
            pl.BlockSpec((hv, tm), lambda i: (0, i)),
        ],
        out_shape=[
            jax.ShapeDtypeStruct((hq, m), BF16),
            jax.ShapeDtypeStruct((m, hq), BF16),
            jax.ShapeDtypeStruct((hv, m), BF16),
        ],
        compiler_params=_params("arbitrary"),
        name="mla_proj",
    )(p_lat, p_lat, p_lat, q_gain, kv_gain, wqt, wk, wvt, cs_tab, cs_tab_t)


def _attn_kernel(*refs, sub, heads, group, dk, lat_keys, ctx_keys):
    if lat_keys:
        qt_ref, kl_ref, vtl_ref, kc_ref, vtc_ref, o_ref, s_a, s_b = refs
    else:
        qt_ref, kc_ref, vtc_ref, o_ref, s_a, s_b = refs
        kl_ref = vtl_ref = None
    n_sub = qt_ref.shape[1] // sub
    n_elems = heads * n_sub
    pieces = [(kl_ref, vtl_ref, c * ATTN_CHUNK, c * ATTN_CHUNK, ATTN_CHUNK) for c in range(lat_keys // ATTN_CHUNK)]
    pieces.append((kc_ref, vtc_ref, 0, lat_keys, ctx_keys))

    def elem(t):
        if isinstance(t, int):
            h, j = t // n_sub, t % n_sub
            return h, h // group, j, (s_a, s_b)[t % 2]
        assert n_sub & (n_sub - 1) == 0 and group & (group - 1) == 0
        h = lax.shift_right_logical(t, n_sub.bit_length() - 1)
        return h, lax.shift_right_logical(h, group.bit_length() - 1), lax.bitwise_and(t, n_sub - 1), None

    def fold8(v):
        return v.reshape(v.shape[0] // 8, 8, v.shape[1])

    def scores(slot, qt, k_rows, row0, rows):
        s = jnp.dot(k_rows, qt, preferred_element_type=F32)
        slot[row0:row0 + rows, :] = s
        return jnp.max(fold8(s), axis=0)

    def weighted(slot, m, vt_cols, row0, rows):
        p = jnp.exp2(slot[row0:row0 + rows, :] - m)
        return jnp.dot(vt_cols, p.astype(BF16), preferred_element_type=F32), jnp.sum(fold8(p), axis=0)

    def stage(cur, prev, m2):
        mrun = jnp.full((8, sub), -jnp.inf, F32)
        lrun = jnp.zeros((8, sub), F32)
        acc = None
        if cur is not None:
            h1, hk1, j1, slot1 = cur
            qt = qt_ref[pl.ds(h1 * dk, dk), pl.ds(j1 * sub, sub)]
        if prev is not None:
            h2, hk2, j2, slot2 = prev
        lead = ATTN_SKEW if (cur is not None and prev is not None) else 0
        for c in range(len(pieces) + lead):
            if cur is not None and c < len(pieces):
                k_ref, _, src0, dst0, rows = pieces[c]
                cm = scores(slot1, qt, k_ref[src0:src0 + rows, pl.ds(hk1 * dk, dk)], dst0, rows)
                mrun = jnp.maximum(mrun, cm)
            if prev is not None and c >= lead:
                _, vt_ref, src0, dst0, rows = pieces[c - lead]
                d, lsum = weighted(slot2, m2, vt_ref[pl.ds(hk2 * HEAD_V, HEAD_V), src0:src0 + rows], dst0, rows)
                acc = d if acc is None else acc + d
                lrun = lrun + lsum
        if prev is not None:
            ot = acc / jnp.sum(lrun, axis=0, keepdims=True)
            o_ref[pl.ds(j2 * sub, sub), pl.ds(h2 * HEAD_V, HEAD_V)] = ot.T.astype(o_ref.dtype)
        return jnp.max(mrun, axis=0, keepdims=True) if cur is not None else None

    m = stage(elem(0), None, None)

    def stage_pair(i, m):
        t = 2 * i + 1
        h0, hk0, j0, _ = elem(t - 1)
        h1, hk1, j1, _ = elem(t)
        h2, hk2, j2, _ = elem(t + 1)
        m = stage((h1, hk1, j1, s_b), (h0, hk0, j0, s_a), m)
        return stage((h2, hk2, j2, s_a), (h1, hk1, j1, s_b), m)

    n_pairs = (n_elems - 1) // 2
    if n_pairs:
        m = lax.fori_loop(0, n_pairs, stage_pair, m)
    for t in range(2 * n_pairs + 1, n_elems):
        m = stage(elem(t), elem(t - 1), m)
    stage(None, elem(n_elems - 1), m)


def _attention(qt, kv_lat, kv_ctx, *, batch, heads, heads_per_step, group, dk, sub, lat_keys, ctx_keys):
    mq = qt.shape[1]
    nq = mq // batch
    hs = heads_per_step
    assert nq % sub == 0 and heads % hs == 0 and (hs % group == 0 or group % hs == 0)
    kvs = max(1, hs // group)
    kv_blk = lambda g: g * hs // (group * kvs)
    in_specs = [pl.BlockSpec((hs * dk, nq), lambda b, g: (g, b))]
    operands = [qt]
    n_keys = ctx_keys
    if kv_lat is not None:
        assert lat_keys % ATTN_CHUNK == 0
        n_keys += lat_keys
        in_specs += [
            pl.BlockSpec((lat_keys, kvs * dk), lambda b, g: (b, kv_blk(g))),
            pl.BlockSpec((kvs * HEAD_V, lat_keys), lambda b, g: (kv_blk(g), b)),
        ]
        operands += list(kv_lat)
    in_specs += [
        pl.BlockSpec((ctx_keys, kvs * dk), lambda b, g: (b, kv_blk(g))),
        pl.BlockSpec((kvs * HEAD_V, ctx_keys), lambda b, g: (kv_blk(g), b)),
    ]
    operands += list(kv_ctx)
    return pl.pallas_call(
        functools.partial(_attn_kernel, sub=sub, heads=hs, group=min(group, hs), dk=dk,
                          lat_keys=lat_keys if kv_lat is not None else 0, ctx_keys=ctx_keys),
        grid=(batch, heads // hs),
        in_specs=in_specs,
        out_specs=pl.BlockSpec((nq, hs * HEAD_V), lambda b, g: (b, g)),
        out_shape=jax.ShapeDtypeStruct((mq, heads * HEAD_V), BF16),
        scratch_shapes=[pltpu.VMEM((n_keys, sub), F32), pltpu.VMEM((n_keys, sub), F32)],
        compiler_params=_params("arbitrary", "arbitrary"),
        name="attention",
    )(*operands)


def _conv_kernel(cb_ref, cc_ref, cu_ref, w_ref, o_ref):
    t = cc_ref[...].astype(F32) * cu_ref[...].astype(F32)
    n = t.shape[0]
    row = lax.broadcasted_iota(jnp.int32, t.shape, 0)
    t_prev = jnp.where(row == 0, 0.0, pltpu.roll(t, 1, 0))
    t_next = jnp.where(row == n - 1, 0.0, pltpu.roll(t, n - 1, 0))
    w = w_ref[...]
    conv = t_prev * w[0:1, :] + t * w[1:2, :] + t_next * w[2:3, :]
    o_ref[...] = (cb_ref[...].astype(F32) * conv).astype(o_ref.dtype)


def _short_conv_gate(p_conv, conv_w, *, batch, width):
    m = p_conv.shape[0]
    n = m // batch
    cblk = CONV_WIDTH // width
    return pl.pallas_call(
        _conv_kernel,
        grid=(batch, cblk),
        in_specs=[
            pl.BlockSpec((n, width), lambda b, j: (b, j)),
            pl.BlockSpec((n, width), lambda b, j: (b, cblk + j)),
            pl.BlockSpec((n, width), lambda b, j: (b, 2 * cblk + j)),
            pl.BlockSpec((3, width), lambda b, j: (0, j)),
        ],
        out_specs=pl.BlockSpec((n, width), lambda b, j: (b, j)),
        out_shape=jax.ShapeDtypeStruct((m, CONV_WIDTH), BF16),
        compiler_params=_params("arbitrary", "arbitrary"),
        name="short_conv",
    )(p_conv, p_conv, p_conv, conv_w)


def _out_kernel(*refs, n_parts):
    parts = refs[:n_parts]
    g_ref, w_ref, x_ref, gate_ref, lng_ref, lnb_ref, o_ref = refs[n_parts:]
    half = x_ref.shape[0] // 2
    ys = []
    for r0 in (0, half):
        y = None
        off = 0
        for p_ref in parts:
            kp = p_ref.shape[1]
            u = p_ref[r0:r0 + half, :] * g_ref[r0:r0 + half, off:off + kp]
            part = jnp.dot(u, w_ref[off:off + kp, :], preferred_element_type=F32)
            y = part if y is None else y + part
            off += kp
        ys.append(y)
    for r0, y in zip((0, half), ys):
        z = ALPHA * x_ref[r0:r0 + half, :] + gate_ref[0] * y
        mu = jnp.mean(z, axis=-1, keepdims=True)
        zc = z - mu
        var = jnp.mean(zc * zc, axis=-1, keepdims=True)
        o_ref[r0:r0 + half, :] = zc * lax.rsqrt(var + EPS) * lng_ref[...] + lnb_ref[...]


def _out_proj_norm(parts, g, w, x, mod3, ln_g, ln_b, *, rows_per_mod, mod_row0, tm):
    m, d = x.shape
    assert m % tm == 0
    if rows_per_mod is None:
        mod_idx = lambda i: mod_row0
    else:
        tiles_per_mod = rows_per_mod // tm
        mod_idx = lambda i: mod_row0 + i // tiles_per_mod
    in_specs = [pl.BlockSpec((tm, p.shape[1]), lambda i: (i, 0)) for p in parts]
    in_specs += [
        pl.BlockSpec((tm, g.shape[1]), lambda i: (i, 0)),
        pl.BlockSpec(w.shape, lambda i: (0, 0), pipeline_mode=pl.Buffered(1)),
        pl.BlockSpec((tm, d), lambda i: (i, 0)),
        pl.BlockSpec((1, 1, d), lambda i: (mod_idx(i), 0, 2)),
        pl.BlockSpec((1, d), lambda i: (0, 0)),
        pl.BlockSpec((1, d), lambda i: (0, 0)),
    ]
    return pl.pallas_call(
        functools.partial(_out_kernel, n_parts=len(parts)),
        grid=(m // tm,),
        in_specs=in_specs,
        out_specs=pl.BlockSpec((tm, d), lambda i: (i, 0)),
        out_shape=jax.ShapeDtypeStruct((m, d), F32),
        compiler_params=_params("arbitrary"),
        name="out_proj_norm",
    )(*parts, g, w, x, mod3, ln_g, ln_b)


def _rope_tables(n, ctx_len):
    rows = n // GRID_W
    pos_row = np.repeat(np.arange(rows, dtype=np.float32), GRID_W)
    pos_col = np.tile(np.arange(GRID_W, dtype=np.float32), rows)

    def angles(pos, half):
        inv = np.float32(ROPE_THETA) ** (-np.arange(half, dtype=np.float32) / np.float32(half))
        return (pos[:, None] * inv[None, :]).astype(np.float32)

    f32 = np.float32
    ar, ac = angles(pos_row, MLA_ROPE // 4), angles(pos_col, MLA_ROPE // 4)
    cs_mla = np.concatenate([np.cos(ar), np.cos(ar), np.cos(ac), np.cos(ac),
                             -np.sin(ar), np.sin(ar), -np.sin(ac), np.sin(ac)], axis=1).astype(f32)
    cs_mla_ctx = np.concatenate([np.ones((ctx_len, MLA_ROPE), f32), np.zeros((ctx_len, MLA_ROPE), f32)], axis=1)
    br, bc = angles(pos_row, GQA_HEAD_DIM // 4), angles(pos_col, GQA_HEAD_DIM // 4)
    cos_gqa = np.concatenate([np.cos(br), np.cos(bc), np.cos(br), np.cos(bc)], axis=1).astype(f32)
    sin_gqa = np.concatenate([-np.sin(br), -np.sin(bc), np.sin(br), np.sin(bc)], axis=1).astype(f32)
    cos_ctx = np.ones((ctx_len, GQA_HEAD_DIM), f32)
    sin_ctx = np.zeros((ctx_len, GQA_HEAD_DIM), f32)
    tabs = dict(mla=cs_mla, mla_ctx=cs_mla_ctx, mla_t=cs_mla.T, mla_ctx_t=cs_mla_ctx.T,
                cos=cos_gqa, sin=sin_gqa, cos_t=cos_gqa.T, sin_t=sin_gqa.T, cos_ctx=cos_ctx, sin_ctx=sin_ctx)
    return {k: jnp.asarray(np.ascontiguousarray(v)) for k, v in tabs.items()}


def kernel(x, c, ctx, c_ctx, w_mod, b_mod, ln_g, ln_b, a_w_in, a_conv_w, a_q_norm, a_w_qb, a_kv_norm,
           a_w_kvb, a_w_out, c_w_in, c_q_norm, c_k_norm, c_w_out):
    batch, n, d = x.shape
    ctx_len = ctx.shape[1]
    assert d == D_MODEL and batch + 1 <= MOD_ROWS
    ctx_mod_row = batch

    tabs = _rope_tables(n, ctx_len)

    cvec = jnp.concatenate([c, c_ctx[None, :], jnp.zeros((MOD_ROWS - batch - 1, d), F32)], axis=0)
    mods = _mod_vectors(cvec, w_mod, b_mod)
    mod0 = mods[0].reshape(MOD_ROWS, 1, 3 * d)
    mod1 = mods[1].reshape(MOD_ROWS, 1, 3 * d)

    xl = x.reshape(batch * n, d)
    xc = ctx.reshape(batch * ctx_len, d)

    w_in0 = a_w_in[0]
    o_c = 3 * CONV_WIDTH
    o_q = o_c + MLA_Q_RANK
    o_kv = o_q + MLA_KV_RANK
    o_g = o_kv + MLA_ROPE
    swap16 = np.concatenate([np.arange(16, 32), np.arange(0, 16), np.arange(48, 64), np.arange(32, 48)])
    w_conv = w_in0[:, :o_c].astype(BF16)
    w_rope = w_in0[:, o_kv:o_g]
    w_lat = jnp.concatenate([w_in0[:, o_c:o_kv], w_rope, w_rope[:, swap16]], axis=1).astype(BF16)
    w_gate0 = w_in0[:, o_g:].astype(BF16)
    wq3 = a_w_qb[0].reshape(MLA_Q_RANK, MLA_HEADS, MLA_NOPE + MLA_ROPE)
    wq_rope = wq3[:, :, MLA_NOPE:]
    wq = jnp.concatenate([wq3[:, :, :MLA_NOPE], wq_rope, wq_rope[:, :, swap16]], axis=2)
    wq_t = wq.reshape(MLA_Q_RANK, MLA_HEADS * MLA_QK_PAD).T.astype(BF16)
    wkv3 = a_w_kvb[0].reshape(MLA_KV_RANK, MLA_HEADS, MLA_NOPE + MLA_V)
    wk0 = wkv3[:, :, :MLA_NOPE].reshape(MLA_KV_RANK, -1).astype(BF16)
    wv0_t = wkv3[:, :, MLA_NOPE:].reshape(MLA_KV_RANK, -1).T.astype(BF16)
    w_out0 = a_w_out[0].astype(BF16)

    w_in1 = c_w_in[0]
    nq1 = GQA_HEADS * GQA_HEAD_DIM
    nk1 = GQA_KV_HEADS * GQA_HEAD_DIM
    perm = np.concatenate([np.arange(0, 32), np.arange(64, 96), np.arange(32, 64), np.arange(96, 128)])
    partner = np.concatenate([np.arange(64, 128), np.arange(0, 64)])
    w_q1_t = w_in1[:, :nq1].reshape(d, GQA_HEADS, GQA_HEAD_DIM)[:, :, perm].reshape(d, nq1).T.astype(BF16)
    w_k1 = w_in1[:, nq1:nq1 + nk1].reshape(d, GQA_KV_HEADS, GQA_HEAD_DIM)[:, :, perm].reshape(d, nk1).astype(BF16)
    w_v1_t = w_in1[:, nq1 + nk1:nq1 + 2 * nk1].T.astype(BF16)
    w_gate1 = w_in1[:, nq1 + 2 * nk1:].astype(BF16)
    q_gain1 = c_q_norm[0][perm].reshape(GQA_HEAD_DIM, 1)
    q_gain1_partner = c_q_norm[0][perm][partner].reshape(GQA_HEAD_DIM, 1)
    k_gain1 = c_k_norm[0][perm].reshape(1, GQA_HEAD_DIM)
    w_out1 = c_w_out[0].astype(BF16)

    lng0, lnb0 = ln_g[0].reshape(1, d), ln_b[0].reshape(1, d)
    lng1, lnb1 = ln_g[1].reshape(1, d), ln_b[1].reshape(1, d)

    lat = dict(rows_per_mod=n, mod_row0=0)
    cxt = dict(rows_per_mod=None, mod_row0=ctx_mod_row)
    tm_l, tm_c = 1024, ctx_len
    q_scale0 = (MLA_NOPE + MLA_ROPE) ** -0.5 * LOG2E
    q_scale1 = GQA_HEAD_DIM ** -0.5 * LOG2E

    def layer0_front(rows, mod_kw, tm_proj, cs_tab, cs_tab_t):
        p_conv = _mod_matmul(rows, mod0, w_conv, tm=tm_l, tn=1024, **mod_kw)
        p_lat = _mod_matmul(rows, mod0, w_lat, tm=tm_l, tn=w_lat.shape[1], **mod_kw)
        gate = _mod_matmul(rows, mod0, w_gate0, tm=tm_l, tn=1024, act="silu", **mod_kw)
        qt, k, vt = _mla_proj(p_lat, a_q_norm[0].reshape(1, -1), a_kv_norm[0].reshape(1, -1), wq_t, wk0, wv0_t,
                              cs_tab, cs_tab_t, tm=tm_proj, q_scale=q_scale0)
        return p_conv, gate, qt, k, vt

    pcl, gl, qtl, kl, vtl = layer0_front(xl, lat, 512, tabs["mla"], tabs["mla_t"])
    pcc, gc, qtc, kc, vtc = layer0_front(xc, cxt, tm_c, tabs["mla_ctx"], tabs["mla_ctx_t"])

    mla = dict(batch=batch, heads=MLA_HEADS, group=1, dk=MLA_QK_PAD, lat_keys=n, ctx_keys=ctx_len)
    attn_l = _attention(qtl, (kl, vtl), (kc, vtc), heads_per_step=2, sub=512, **mla)
    attn_c = _attention(qtc, None, (kc, vtc), heads_per_step=MLA_HEADS, sub=ctx_len, **mla)

    conv_l = _short_conv_gate(pcl, a_conv_w[0], batch=batch, width=2 * LANES)
    conv_c = _short_conv_gate(pcc, a_conv_w[0], batch=batch, width=CONV_WIDTH)

    x1 = _out_proj_norm([conv_l, attn_l], gl, w_out0, xl, mod0, lng0, lnb0, tm=512, **lat)
    h1 = _out_proj_norm([conv_c, attn_c], gc, w_out0, xc, mod0, lng0, lnb0, tm=256, **cxt)

    qt1 = _q_heads(x1, mod1, w_q1_t, q_gain1, q_gain1_partner, tabs["cos_t"], tabs["sin_t"],
                   tm=tm_l, tn=1024, q_scale=q_scale1, **lat)
    k1, vt1 = _kv_heads(x1, mod1, w_k1, w_v1_t, k_gain1, tabs["cos"], tabs["sin"], tm=tm_l, **lat)
    g1 = _mod_matmul(x1, mod1, w_gate1, tm=tm_l, tn=1024, act="silu", **lat)
    kc1, vtc1 = _kv_heads(h1, mod1, w_k1, w_v1_t, k_gain1, tabs["cos_ctx"], tabs["sin_ctx"], tm=tm_c, **cxt)

    attn1 = _attention(qt1, (k1, vt1), (kc1, vtc1), batch=batch, heads=GQA_HEADS, heads_per_step=GQA_GROUP,
                       group=GQA_GROUP, dk=GQA_HEAD_DIM, sub=512, lat_keys=n, ctx_keys=ctx_len)

    out = _out_proj_norm([attn1], g1, w_out1, x1, mod1, lng1, lnb1, tm=512, **lat)
    return out.reshape(batch, n, d)
```

```python
import functools
import math

import jax
import jax.numpy as jnp
import numpy as np
from jax import lax
from jax.experimental import pallas as pl
from jax.experimental.pallas import tpu as pltpu

F32 = jnp.float32
BF16 = jnp.bfloat16

D_MODEL = 2048
DEPTH = 2
GRID_W = 64
ROPE_THETA = 10000.0
EPS = 1e-6
ALPHA = (2 * DEPTH) ** 0.25

CONV_WIDTH = D_MODEL // 2
MLA_HEADS = D_MODEL // 256
MLA_Q_RANK = 512
MLA_KV_RANK = 512
MLA_NOPE = 128
MLA_ROPE = 64
MLA_V = 128
MLA_QK_PAD = 256
GQA_HEAD_DIM = 128
GQA_HEADS = D_MODEL // GQA_HEAD_DIM
GQA_KV_HEADS = GQA_HEADS // 4
GQA_GROUP = GQA_HEADS // GQA_KV_HEADS

LANES = 128
HEAD_V = 128
MOD_ROWS = 16
VMEM_LIMIT = 56 * 1024 * 1024
LOG2E = math.log2(math.e)
ATTN_CHUNK = 256
ATTN_SKEW = 1


def _params(*sem):
    return pltpu.CompilerParams(dimension_semantics=sem, vmem_limit_bytes=VMEM_LIMIT)


def _silu(v):
    return v * jax.nn.sigmoid(v)


_NT_DIMS = (((1,), (1,)), ((), ()))


def _dot_nt(a, b):
    return lax.dot_general(a, b, _NT_DIMS, preferred_element_type=F32)


def _mod_kernel(c_ref, w_ref, b_ref, o_ref):
    s = _silu(c_ref[...]).astype(BF16)
    o_ref[0] = jnp.dot(s, w_ref[0].astype(BF16), preferred_element_type=F32) + b_ref[0]


def _mod_vectors(cvec, w_mod, b_mod):
    depth, d, n3 = w_mod.shape
    tn = 768
    return pl.pallas_call(
        _mod_kernel,
        grid=(depth, n3 // tn),
        in_specs=[
            pl.BlockSpec((MOD_ROWS, d), lambda l, j: (0, 0)),
            pl.BlockSpec((1, d, tn), lambda l, j: (l, 0, j)),
            pl.BlockSpec((1, 1, tn), lambda l, j: (l, 0, j)),
        ],
        out_specs=pl.BlockSpec((1, MOD_ROWS, tn), lambda l, j: (l, 0, j)),
        out_shape=jax.ShapeDtypeStruct((depth, MOD_ROWS, n3), F32),
        compiler_params=_params("arbitrary", "arbitrary"),
        name="mod_vectors",
    )(cvec, w_mod, b_mod.reshape(depth, 1, n3))


def _rms_rope(y, g, cos, sin):
    ms = jnp.mean(y * y, axis=-1, keepdims=True)
    yn = y * lax.rsqrt(ms + EPS) * g
    return yn * cos + pltpu.roll(yn, LANES // 2, 1) * sin


def _modulate(x_ref, sh_ref, sc_ref, xin_ref):
    @pl.when(pl.program_id(1) == 0)
    def _():
        xin_ref[...] = (x_ref[...] * (1.0 + sc_ref[0]) + sh_ref[0]).astype(BF16)


def _plain_kernel(x_ref, sh_ref, sc_ref, w_ref, o_ref, xin_ref, *, act):
    _modulate(x_ref, sh_ref, sc_ref, xin_ref)
    half = xin_ref.shape[0] // 2
    w = w_ref[...]
    accs = [jnp.dot(xin_ref[r0:r0 + half, :], w, preferred_element_type=F32) for r0 in (0, half)]
    for r0, acc in zip((0, half), accs):
        o_ref[r0:r0 + half, :] = (_silu(acc) if act == "silu" else acc).astype(o_ref.dtype)


def _q_heads_kernel(x_ref, sh_ref, sc_ref, wt_ref, g_ref, gp_ref, cos_ref, sin_ref, qt_ref, xin_ref, *, q_scale):
    _modulate(x_ref, sh_ref, sc_ref, xin_ref)
    acc = _dot_nt(wt_ref[...], xin_ref[...])
    a = g_ref[...] * cos_ref[...]
    b = gp_ref[...] * sin_ref[...]
    half = LANES // 2
    for h in range(acc.shape[0] // LANES):
        y = acc[h * LANES:(h + 1) * LANES]
        yn = y * (lax.rsqrt(jnp.mean(y * y, axis=0, keepdims=True) + EPS) * q_scale)
        partner = jnp.concatenate([yn[half:], yn[:half]], axis=0)
        qt_ref[h * LANES:(h + 1) * LANES, :] = (yn * a + partner * b).astype(qt_ref.dtype)


def _kv_heads_kernel(x_ref, sh_ref, sc_ref, wk_ref, wvt_ref, g_ref, cos_ref, sin_ref, k_ref, vt_ref, xin_ref):
    _modulate(x_ref, sh_ref, sc_ref, xin_ref)
    xin = xin_ref[...]
    g, cos, sin = g_ref[...], cos_ref[...], sin_ref[...]
    yk = jnp.dot(xin, wk_ref[...], preferred_element_type=F32)
    for h in range(yk.shape[1] // LANES):
        k_ref[:, h * LANES:(h + 1) * LANES] = _rms_rope(
            yk[:, h * LANES:(h + 1) * LANES], g, cos, sin).astype(k_ref.dtype)
    vt_ref[...] = _dot_nt(wvt_ref[...], xin).astype(vt_ref.dtype)


def _mod_specs(d, tm, rows_per_mod, mod_row0):
    if rows_per_mod is None:
        mod_idx = lambda i: mod_row0
    else:
        assert rows_per_mod % tm == 0
        tiles_per_mod = rows_per_mod // tm
        mod_idx = lambda i: mod_row0 + i // tiles_per_mod
    return [
        pl.BlockSpec((tm, d), lambda i, j: (i, 0)),
        pl.BlockSpec((1, 1, d), lambda i, j: (mod_idx(i), 0, 0)),
        pl.BlockSpec((1, 1, d), lambda i, j: (mod_idx(i), 0, 1)),
    ]


def _mod_matmul(x, mod3, w, *, rows_per_mod, mod_row0, tm, tn, act="none"):
    m, d = x.shape
    n = w.shape[1]
    assert m % tm == 0 and n % tn == 0
    return pl.pallas_call(
        functools.partial(_plain_kernel, act=act),
        grid=(m // tm, n // tn),
        in_specs=_mod_specs(d, tm, rows_per_mod, mod_row0) + [pl.BlockSpec((d, tn), lambda i, j: (0, j))],
        out_specs=pl.BlockSpec((tm, tn), lambda i, j: (i, j)),
        out_shape=jax.ShapeDtypeStruct((m, n), BF16),
        scratch_shapes=[pltpu.VMEM((tm, d), BF16)],
        compiler_params=_params("arbitrary", "arbitrary"),
        name="mod_matmul_" + act,
    )(x, mod3, mod3, w)


def _q_heads(x, mod3, wt, gain, gain_partner, cos_t, sin_t, *, rows_per_mod, mod_row0, tm, tn, q_scale):
    m, d = x.shape
    n = wt.shape[0]
    pos = cos_t.shape[1]
    assert m % tm == 0 and n % tn == 0 and pos % tm == 0
    tiles_per_seq = pos // tm
    col = pl.BlockSpec((LANES, 1), lambda i, j: (0, 0))
    tab = pl.BlockSpec((LANES, tm), lambda i, j: (0, i % tiles_per_seq))
    return pl.pallas_call(
        functools.partial(_q_heads_kernel, q_scale=q_scale),
        grid=(m // tm, n // tn),
        in_specs=_mod_specs(d, tm, rows_per_mod, mod_row0) + [
            pl.BlockSpec((tn, d), lambda i, j: (j, 0)), col, col, tab, tab],
        out_specs=pl.BlockSpec((tn, tm), lambda i, j: (j, i)),
        out_shape=jax.ShapeDtypeStruct((n, m), BF16),
        scratch_shapes=[pltpu.VMEM((tm, d), BF16)],
        compiler_params=_params("arbitrary", "arbitrary"),
        name="q_heads",
    )(x, mod3, mod3, wt, gain, gain_partner, cos_t, sin_t)


def _kv_heads(x, mod3, wk, wvt, gain, cos, sin, *, rows_per_mod, mod_row0, tm):
    m, d = x.shape
    nk = wk.shape[1]
    nv = wvt.shape[0]
    pos = cos.shape[0]
    assert m % tm == 0 and pos % tm == 0
    tiles_per_seq = pos // tm
    tab = pl.BlockSpec((tm, LANES), lambda i, j: (i % tiles_per_seq, 0))
    return pl.pallas_call(
        _kv_heads_kernel,
        grid=(m // tm, 1),
        in_specs=_mod_specs(d, tm, rows_per_mod, mod_row0) + [
            pl.BlockSpec((d, nk), lambda i, j: (0, 0)),
            pl.BlockSpec((nv, d), lambda i, j: (0, 0)),
            pl.BlockSpec((1, LANES), lambda i, j: (0, 0)), tab, tab],
        out_specs=[pl.BlockSpec((tm, nk), lambda i, j: (i, 0)), pl.BlockSpec((nv, tm), lambda i, j: (0, i))],
        out_shape=[jax.ShapeDtypeStruct((m, nk), BF16), jax.ShapeDtypeStruct((nv, m), BF16)],
        scratch_shapes=[pltpu.VMEM((tm, d), BF16)],
        compiler_params=_params("arbitrary", "arbitrary"),
        name="kv_heads",
    )(x, mod3, mod3, wk, wvt, gain, cos, sin)


def _mla_proj_kernel(ql_ref, kvl_ref, kr_ref, qg_ref, kvg_ref, wqt_ref, wk_ref, wvt_ref, cs_ref, cst_ref,
                     qt_ref, k_ref, vt_ref, *, q_scale):
    def rms(v, g):
        vf = v.astype(F32)
        return (vf * lax.rsqrt(jnp.mean(vf * vf, axis=-1, keepdims=True) + EPS) * g).astype(BF16)

    yqt = _dot_nt(wqt_ref[...], rms(ql_ref[...], qg_ref[...]))
    cst = cst_ref[...]
    for h in range(MLA_HEADS):
        lo = h * MLA_QK_PAD
        qt_ref[lo:lo + MLA_NOPE, :] = (yqt[lo:lo + MLA_NOPE] * q_scale).astype(qt_ref.dtype)
        t = yqt[lo + MLA_NOPE:lo + MLA_QK_PAD] * cst
        rope = (t[:MLA_ROPE] + t[MLA_ROPE:]) * q_scale
        qt_ref[lo + MLA_NOPE:lo + MLA_NOPE + MLA_ROPE, :] = rope.astype(qt_ref.dtype)
        qt_ref[lo + MLA_NOPE + MLA_ROPE:lo + MLA_QK_PAD, :] = jnp.zeros((MLA_ROPE, rope.shape[1]), qt_ref.dtype)

    rkv = rms(kvl_ref[...], kvg_ref[...])
    yk = jnp.dot(rkv, wk_ref[...], preferred_element_type=F32)
    t = kr_ref[...].astype(F32) * cs_ref[...]
    kr = t + pltpu.roll(t, LANES // 2, 1)
    lane = lax.broadcasted_iota(jnp.int32, kr.shape, 1)
    kr = jnp.where(lane < MLA_ROPE, kr, 0.0).astype(k_ref.dtype)
    for h in range(MLA_HEADS):
        lo = h * MLA_QK_PAD
        k_ref[:, lo:lo + MLA_NOPE] = yk[:, h * MLA_NOPE:(h + 1) * MLA_NOPE].astype(k_ref.dtype)
        k_ref[:, lo + MLA_NOPE:lo + MLA_QK_PAD] = kr
    vt_ref[...] = _dot_nt(wvt_ref[...], rkv).astype(vt_ref.dtype)


def _mla_proj(p_lat, q_gain, kv_gain, wqt, wk, wvt, cs_tab, cs_tab_t, *, tm, q_scale):
    m = p_lat.shape[0]
    pos_rows = cs_tab.shape[0]
    assert m % tm == 0 and pos_rows % tm == 0
    tiles_per_seq = pos_rows // tm
    hq = MLA_HEADS * MLA_QK_PAD
    hv = MLA_HEADS * MLA_V
    rope_blk = (MLA_Q_RANK + MLA_KV_RANK) // LANES
    return pl.pallas_call(
        functools.partial(_mla_proj_kernel, q_scale=q_scale),
        grid=(m // tm,),
        in_specs=[
            pl.BlockSpec((tm, MLA_Q_RANK), lambda i: (i, 0)),
            pl.BlockSpec((tm, MLA_KV_RANK), lambda i: (i, 1)),
            pl.BlockSpec((tm, LANES), lambda i: (i, rope_blk)),
            pl.BlockSpec((1, MLA_Q_RANK), lambda i: (0, 0)),
            pl.BlockSpec((1, MLA_KV_RANK), lambda i: (0, 0)),
            pl.BlockSpec((hq, MLA_Q_RANK), lambda i: (0, 0)),
            pl.BlockSpec((MLA_KV_RANK, hv), lambda i: (0, 0)),
            pl.BlockSpec((hv, MLA_KV_RANK), lambda i: (0, 0)),
            pl.BlockSpec((tm, LANES), lambda i: (i % tiles_per_seq, 0)),
            pl.BlockSpec((LANES, tm), lambda i: (0, i % tiles_per_seq)),
        ],
        out_specs=[
            pl.BlockSpec((hq, tm), lambda i: (0, i)),
            pl.BlockSpec((tm, hq), lambda i: (i, 0)),
            pl.BlockSpec((hv, tm), lambda i: (0, i)),
        ],
        out_shape=[
            jax.ShapeDtypeStruct((hq, m), BF16),
            jax.ShapeDtypeStruct((m, hq), BF16),
            jax.ShapeDtypeStruct((hv, m), BF16),
        ],
        compiler_params=_params("arbitrary"),
        name="mla_proj",
    )(p_lat, p_lat, p_lat, q_gain, kv_gain, wqt, wk, wvt, cs_tab, cs_tab_t)


def _attn_kernel(*refs, sub, heads, group, dk, lat_keys, ctx_keys, stages_per_trip):
    if lat_keys:
        qt_ref, kl_ref, vtl_ref, kc_ref, vtc_ref, o_ref, s_a, s_b = refs
    else:
        qt_ref, kc_ref, vtc_ref, o_ref, s_a, s_b = refs
        kl_ref = vtl_ref = None
    n_sub = qt_ref.shape[1] // sub
    n_elems = heads * n_sub
    pieces = [(kl_ref, vtl_ref, c * ATTN_CHUNK, c * ATTN_CHUNK, ATTN_CHUNK) for c in range(lat_keys // ATTN_CHUNK)]
    pieces.append((kc_ref, vtc_ref, 0, lat_keys, ctx_keys))

    def elem(t):
        if isinstance(t, int):
            h, j = t // n_sub, t % n_sub
            return h, h // group, j, (s_a, s_b)[t % 2]
        assert n_sub & (n_sub - 1) == 0 and group & (group - 1) == 0
        h = lax.shift_right_logical(t, n_sub.bit_length() - 1)
        return h, lax.shift_right_logical(h, group.bit_length() - 1), lax.bitwise_and(t, n_sub - 1), None

    def fold8(v):
        return v.reshape(v.shape[0] // 8, 8, v.shape[1])

    def scores(slot, qt, k_rows, row0, rows):
        s = jnp.dot(k_rows, qt, preferred_element_type=F32)
        slot[row0:row0 + rows, :] = s
        return jnp.max(fold8(s), axis=0)

    def weighted(slot, m, vt_cols, row0, rows):
        p = jnp.exp2(slot[row0:row0 + rows, :] - m)
        return jnp.dot(vt_cols, p.astype(BF16), preferred_element_type=F32), jnp.sum(fold8(p), axis=0)

    def stage(cur, prev, m2):
        mrun = jnp.full((8, sub), -jnp.inf, F32)
        lrun = jnp.zeros((8, sub), F32)
        acc = None
        if cur is not None:
            h1, hk1, j1, slot1 = cur
            qt = qt_ref[pl.ds(h1 * dk, dk), pl.ds(j1 * sub, sub)]
        if prev is not None:
            h2, hk2, j2, slot2 = prev
        lead = ATTN_SKEW if (cur is not None and prev is not None) else 0
        for c in range(len(pieces) + lead):
            if cur is not None and c < len(pieces):
                k_ref, _, src0, dst0, rows = pieces[c]
                cm = scores(slot1, qt, k_ref[src0:src0 + rows, pl.ds(hk1 * dk, dk)], dst0, rows)
                mrun = jnp.maximum(mrun, cm)
            if prev is not None and c >= lead:
                _, vt_ref, src0, dst0, rows = pieces[c - lead]
                d, lsum = weighted(slot2, m2, vt_ref[pl.ds(hk2 * HEAD_V, HEAD_V), src0:src0 + rows], dst0, rows)
                acc = d if acc is None else acc + d
                lrun = lrun + lsum
        if prev is not None:
            ot = acc / jnp.sum(lrun, axis=0, keepdims=True)
            o_ref[pl.ds(j2 * sub, sub), pl.ds(h2 * HEAD_V, HEAD_V)] = ot.T.astype(o_ref.dtype)
        return jnp.max(mrun, axis=0, keepdims=True) if cur is not None else None

    m = stage(elem(0), None, None)

    assert stages_per_trip % 2 == 0

    def stage_group(i, m):
        t0 = stages_per_trip * i
        prev = elem(t0)[:3] + (s_a,)
        for d in range(1, stages_per_trip + 1):
            cur = elem(t0 + d)[:3] + ((s_a, s_b)[d % 2],)
            m = stage(cur, prev, m)
            prev = cur
        return m

    n_trips = (n_elems - 1) // stages_per_trip
    if n_trips:
        m = lax.fori_loop(0, n_trips, stage_group, m)
    for t in range(stages_per_trip * n_trips + 1, n_elems):
        m = stage(elem(t), elem(t - 1), m)
    stage(None, elem(n_elems - 1), m)


def _attention(qt, kv_lat, kv_ctx, *, batch, heads, heads_per_step, group, dk, sub, lat_keys, ctx_keys,
               stages_per_trip):
    mq = qt.shape[1]
    nq = mq // batch
    hs = heads_per_step
    assert nq % sub == 0 and heads % hs == 0 and (hs % group == 0 or group % hs == 0)
    kvs = max(1, hs // group)
    kv_blk = lambda g: g * hs // (group * kvs)
    in_specs = [pl.BlockSpec((hs * dk, nq), lambda b, g: (g, b))]
    operands = [qt]
    n_keys = ctx_keys
    if kv_lat is not None:
        assert lat_keys % ATTN_CHUNK == 0
        n_keys += lat_keys
        in_specs += [
            pl.BlockSpec((lat_keys, kvs * dk), lambda b, g: (b, kv_blk(g))),
            pl.BlockSpec((kvs * HEAD_V, lat_keys), lambda b, g: (kv_blk(g), b)),
        ]
        operands += list(kv_lat)
    in_specs += [
        pl.BlockSpec((ctx_keys, kvs * dk), lambda b, g: (b, kv_blk(g))),
        pl.BlockSpec((kvs * HEAD_V, ctx_keys), lambda b, g: (kv_blk(g), b)),
    ]
    operands += list(kv_ctx)
    return pl.pallas_call(
        functools.partial(_attn_kernel, sub=sub, heads=hs, group=min(group, hs), dk=dk,
                          lat_keys=lat_keys if kv_lat is not None else 0, ctx_keys=ctx_keys,
                          stages_per_trip=stages_per_trip),
        grid=(batch, heads // hs),
        in_specs=in_specs,
        out_specs=pl.BlockSpec((nq, hs * HEAD_V), lambda b, g: (b, g)),
        out_shape=jax.ShapeDtypeStruct((mq, heads * HEAD_V), BF16),
        scratch_shapes=[pltpu.VMEM((n_keys, sub), F32), pltpu.VMEM((n_keys, sub), F32)],
        compiler_params=_params("arbitrary", "arbitrary"),
        name="attention",
    )(*operands)


def _conv_kernel(cb_ref, cc_ref, cu_ref, w_ref, o_ref):
    t = cc_ref[...].astype(F32) * cu_ref[...].astype(F32)
    n = t.shape[0]
    row = lax.broadcasted_iota(jnp.int32, t.shape, 0)
    t_prev = jnp.where(row == 0, 0.0, pltpu.roll(t, 1, 0))
    t_next = jnp.where(row == n - 1, 0.0, pltpu.roll(t, n - 1, 0))
    w = w_ref[...]
    conv = t_prev * w[0:1, :] + t * w[1:2, :] + t_next * w[2:3, :]
    o_ref[...] = (cb_ref[...].astype(F32) * conv).astype(o_ref.dtype)


def _short_conv_gate(p_conv, conv_w, *, batch, width):
    m = p_conv.shape[0]
    n = m // batch
    cblk = CONV_WIDTH // width
    return pl.pallas_call(
        _conv_kernel,
        grid=(batch, cblk),
        in_specs=[
            pl.BlockSpec((n, width), lambda b, j: (b, j)),
            pl.BlockSpec((n, width), lambda b, j: (b, cblk + j)),
            pl.BlockSpec((n, width), lambda b, j: (b, 2 * cblk + j)),
            pl.BlockSpec((3, width), lambda b, j: (0, j)),
        ],
        out_specs=pl.BlockSpec((n, width), lambda b, j: (b, j)),
        out_shape=jax.ShapeDtypeStruct((m, CONV_WIDTH), BF16),
        compiler_params=_params("arbitrary", "arbitrary"),
        name="short_conv",
    )(p_conv, p_conv, p_conv, conv_w)


def _out_kernel(*refs, n_parts):
    parts = refs[:n_parts]
    g_ref, w_ref, x_ref, gate_ref, lng_ref, lnb_ref, o_ref = refs[n_parts:]
    half = x_ref.shape[0] // 2
    ys = []
    for r0 in (0, half):
        y = None
        off = 0
        for p_ref in parts:
            kp = p_ref.shape[1]
            u = p_ref[r0:r0 + half, :] * g_ref[r0:r0 + half, off:off + kp]
            part = jnp.dot(u, w_ref[off:off + kp, :], preferred_element_type=F32)
            y = part if y is None else y + part
            off += kp
        ys.append(y)
    for r0, y in zip((0, half), ys):
        z = ALPHA * x_ref[r0:r0 + half, :] + gate_ref[0] * y
        mu = jnp.mean(z, axis=-1, keepdims=True)
        zc = z - mu
        var = jnp.mean(zc * zc, axis=-1, keepdims=True)
        o_ref[r0:r0 + half, :] = zc * lax.rsqrt(var + EPS) * lng_ref[...] + lnb_ref[...]


def _out_proj_norm(parts, g, w, x, mod3, ln_g, ln_b, *, rows_per_mod, mod_row0, tm):
    m, d = x.shape
    assert m % tm == 0
    if rows_per_mod is None:
        mod_idx = lambda i: mod_row0
    else:
        tiles_per_mod = rows_per_mod // tm
        mod_idx = lambda i: mod_row0 + i // tiles_per_mod
    in_specs = [pl.BlockSpec((tm, p.shape[1]), lambda i: (i, 0)) for p in parts]
    in_specs += [
        pl.BlockSpec((tm, g.shape[1]), lambda i: (i, 0)),
        pl.BlockSpec(w.shape, lambda i: (0, 0), pipeline_mode=pl.Buffered(1)),
        pl.BlockSpec((tm, d), lambda i: (i, 0)),
        pl.BlockSpec((1, 1, d), lambda i: (mod_idx(i), 0, 2)),
        pl.BlockSpec((1, d), lambda i: (0, 0)),
        pl.BlockSpec((1, d), lambda i: (0, 0)),
    ]
    return pl.pallas_call(
        functools.partial(_out_kernel, n_parts=len(parts)),
        grid=(m // tm,),
        in_specs=in_specs,
        out_specs=pl.BlockSpec((tm, d), lambda i: (i, 0)),
        out_shape=jax.ShapeDtypeStruct((m, d), F32),
        compiler_params=_params("arbitrary"),
        name="out_proj_norm",
    )(*parts, g, w, x, mod3, ln_g, ln_b)


def _rope_tables(n, ctx_len):
    rows = n // GRID_W
    pos_row = np.repeat(np.arange(rows, dtype=np.float32), GRID_W)
    pos_col = np.tile(np.arange(GRID_W, dtype=np.float32), rows)

    def angles(pos, half):
        inv = np.float32(ROPE_THETA) ** (-np.arange(half, dtype=np.float32) / np.float32(half))
        return (pos[:, None] * inv[None, :]).astype(np.float32)

    f32 = np.float32
    ar, ac = angles(pos_row, MLA_ROPE // 4), angles(pos_col, MLA_ROPE // 4)
    cs_mla = np.concatenate([np.cos(ar), np.cos(ar), np.cos(ac), np.cos(ac),
                             -np.sin(ar), np.sin(ar), -np.sin(ac), np.sin(ac)], axis=1).astype(f32)
    cs_mla_ctx = np.concatenate([np.ones((ctx_len, MLA_ROPE), f32), np.zeros((ctx_len, MLA_ROPE), f32)], axis=1)
    br, bc = angles(pos_row, GQA_HEAD_DIM // 4), angles(pos_col, GQA_HEAD_DIM // 4)
    cos_gqa = np.concatenate([np.cos(br), np.cos(bc), np.cos(br), np.cos(bc)], axis=1).astype(f32)
    sin_gqa = np.concatenate([-np.sin(br), -np.sin(bc), np.sin(br), np.sin(bc)], axis=1).astype(f32)
    cos_ctx = np.ones((ctx_len, GQA_HEAD_DIM), f32)
    sin_ctx = np.zeros((ctx_len, GQA_HEAD_DIM), f32)
    tabs = dict(mla=cs_mla, mla_ctx=cs_mla_ctx, mla_t=cs_mla.T, mla_ctx_t=cs_mla_ctx.T,
                cos=cos_gqa, sin=sin_gqa, cos_t=cos_gqa.T, sin_t=sin_gqa.T, cos_ctx=cos_ctx, sin_ctx=sin_ctx)
    return {k: jnp.asarray(np.ascontiguousarray(v)) for k, v in tabs.items()}


def kernel(x, c, ctx, c_ctx, w_mod, b_mod, ln_g, ln_b, a_w_in, a_conv_w, a_q_norm, a_w_qb, a_kv_norm,
           a_w_kvb, a_w_out, c_w_in, c_q_norm, c_k_norm, c_w_out):
    batch, n, d = x.shape
    ctx_len = ctx.shape[1]
    assert d == D_MODEL and batch + 1 <= MOD_ROWS
    ctx_mod_row = batch

    tabs = _rope_tables(n, ctx_len)

    cvec = jnp.concatenate([c, c_ctx[None, :], jnp.zeros((MOD_ROWS - batch - 1, d), F32)], axis=0)
    mods = _mod_vectors(cvec, w_mod, b_mod)
    mod0 = mods[0].reshape(MOD_ROWS, 1, 3 * d)
    mod1 = mods[1].reshape(MOD_ROWS, 1, 3 * d)

    xl = x.reshape(batch * n, d)
    xc = ctx.reshape(batch * ctx_len, d)

    w_in0 = a_w_in[0]
    o_c = 3 * CONV_WIDTH
    o_q = o_c + MLA_Q_RANK
    o_kv = o_q + MLA_KV_RANK
    o_g = o_kv + MLA_ROPE
    swap16 = np.concatenate([np.arange(16, 32), np.arange(0, 16), np.arange(48, 64), np.arange(32, 48)])
    w_conv = w_in0[:, :o_c].astype(BF16)
    w_rope = w_in0[:, o_kv:o_g]
    w_lat = jnp.concatenate([w_in0[:, o_c:o_kv], w_rope, w_rope[:, swap16]], axis=1).astype(BF16)
    w_gate0 = w_in0[:, o_g:].astype(BF16)
    wq3 = a_w_qb[0].reshape(MLA_Q_RANK, MLA_HEADS, MLA_NOPE + MLA_ROPE)
    wq_rope = wq3[:, :, MLA_NOPE:]
    wq = jnp.concatenate([wq3[:, :, :MLA_NOPE], wq_rope, wq_rope[:, :, swap16]], axis=2)
    wq_t = wq.reshape(MLA_Q_RANK, MLA_HEADS * MLA_QK_PAD).T.astype(BF16)
    wkv3 = a_w_kvb[0].reshape(MLA_KV_RANK, MLA_HEADS, MLA_NOPE + MLA_V)
    wk0 = wkv3[:, :, :MLA_NOPE].reshape(MLA_KV_RANK, -1).astype(BF16)
    wv0_t = wkv3[:, :, MLA_NOPE:].reshape(MLA_KV_RANK, -1).T.astype(BF16)
    w_out0 = a_w_out[0].astype(BF16)

    w_in1 = c_w_in[0]
    nq1 = GQA_HEADS * GQA_HEAD_DIM
    nk1 = GQA_KV_HEADS * GQA_HEAD_DIM
    perm = np.concatenate([np.arange(0, 32), np.arange(64, 96), np.arange(32, 64), np.arange(96, 128)])
    partner = np.concatenate([np.arange(64, 128), np.arange(0, 64)])
    w_q1_t = w_in1[:, :nq1].reshape(d, GQA_HEADS, GQA_HEAD_DIM)[:, :, perm].reshape(d, nq1).T.astype(BF16)
    w_k1 = w_in1[:, nq1:nq1 + nk1].reshape(d, GQA_KV_HEADS, GQA_HEAD_DIM)[:, :, perm].reshape(d, nk1).astype(BF16)
    w_v1_t = w_in1[:, nq1 + nk1:nq1 + 2 * nk1].T.astype(BF16)
    w_gate1 = w_in1[:, nq1 + 2 * nk1:].astype(BF16)
    q_gain1 = c_q_norm[0][perm].reshape(GQA_HEAD_DIM, 1)
    q_gain1_partner = c_q_norm[0][perm][partner].reshape(GQA_HEAD_DIM, 1)
    k_gain1 = c_k_norm[0][perm].reshape(1, GQA_HEAD_DIM)
    w_out1 = c_w_out[0].astype(BF16)

    lng0, lnb0 = ln_g[0].reshape(1, d), ln_b[0].reshape(1, d)
    lng1, lnb1 = ln_g[1].reshape(1, d), ln_b[1].reshape(1, d)

    lat = dict(rows_per_mod=n, mod_row0=0)
    cxt = dict(rows_per_mod=None, mod_row0=ctx_mod_row)
    tm_l, tm_c = 1024, ctx_len
    q_scale0 = (MLA_NOPE + MLA_ROPE) ** -0.5 * LOG2E
    q_scale1 = GQA_HEAD_DIM ** -0.5 * LOG2E

    def layer0_front(rows, mod_kw, tm_proj, cs_tab, cs_tab_t):
        p_conv = _mod_matmul(rows, mod0, w_conv, tm=tm_l, tn=1024, **mod_kw)
        p_lat = _mod_matmul(rows, mod0, w_lat, tm=tm_l, tn=w_lat.shape[1], **mod_kw)
        gate = _mod_matmul(rows, mod0, w_gate0, tm=tm_l, tn=1024, act="silu", **mod_kw)
        qt, k, vt = _mla_proj(p_lat, a_q_norm[0].reshape(1, -1), a_kv_norm[0].reshape(1, -1), wq_t, wk0, wv0_t,
                              cs_tab, cs_tab_t, tm=tm_proj, q_scale=q_scale0)
        return p_conv, gate, qt, k, vt

    pcl, gl, qtl, kl, vtl = layer0_front(xl, lat, 512, tabs["mla"], tabs["mla_t"])
    pcc, gc, qtc, kc, vtc = layer0_front(xc, cxt, tm_c, tabs["mla_ctx"], tabs["mla_ctx_t"])

    mla = dict(batch=batch, heads=MLA_HEADS, group=1, dk=MLA_QK_PAD, lat_keys=n, ctx_keys=ctx_len)
    attn_l = _attention(qtl, (kl, vtl), (kc, vtc), heads_per_step=2, sub=512, stages_per_trip=4, **mla)
    attn_c = _attention(qtc, None, (kc, vtc), heads_per_step=MLA_HEADS, sub=ctx_len, stages_per_trip=2, **mla)

    conv_l = _short_conv_gate(pcl, a_conv_w[0], batch=batch, width=2 * LANES)
    conv_c = _short_conv_gate(pcc, a_conv_w[0], batch=batch, width=CONV_WIDTH)

    x1 = _out_proj_norm([conv_l, attn_l], gl, w_out0, xl, mod0, lng0, lnb0, tm=512, **lat)
    h1 = _out_proj_norm([conv_c, attn_c], gc, w_out0, xc, mod0, lng0, lnb0, tm=256, **cxt)

    qt1 = _q_heads(x1, mod1, w_q1_t, q_gain1, q_gain1_partner, tabs["cos_t"], tabs["sin_t"],
                   tm=tm_l, tn=1024, q_scale=q_scale1, **lat)
    k1, vt1 = _kv_heads(x1, mod1, w_k1, w_v1_t, k_gain1, tabs["cos"], tabs["sin"], tm=tm_l, **lat)
    g1 = _mod_matmul(x1, mod1, w_gate1, tm=tm_l, tn=1024, act="silu", **lat)
    kc1, vtc1 = _kv_heads(h1, mod1, w_k1, w_v1_t, k_gain1, tabs["cos_ctx"], tabs["sin_ctx"], tm=tm_c, **cxt)

    attn1 = _attention(qt1, (k1, vt1), (kc1, vtc1), batch=batch, heads=GQA_HEADS, heads_per_step=GQA_GROUP,
                       group=GQA_GROUP, dk=GQA_HEAD_DIM, sub=512, lat_keys=n, ctx_keys=ctx_len, stages_per_trip=6)

    out = _out_proj_norm([attn1], g1, w_out1, x1, mod1, lng1, lnb1, tm=512, **lat)
    return out.reshape(batch, n, d)
```

```python
import functools
import math

import jax
import jax.numpy as jnp
import numpy as np
from jax import lax
from jax.experimental import pallas as pl
from jax.experimental.pallas import tpu as pltpu

F32 = jnp.float32
BF16 = jnp.bfloat16

D_MODEL = 2048
DEPTH = 2
GRID_W = 64
ROPE_THETA = 10000.0
EPS = 1e-6
ALPHA = (2 * DEPTH) ** 0.25

CONV_WIDTH = D_MODEL // 2
MLA_HEADS = D_MODEL // 256
MLA_Q_RANK = 512
MLA_KV_RANK = 512
MLA_NOPE = 128
MLA_ROPE = 64
MLA_V = 128
MLA_QK_PAD = 256
GQA_HEAD_DIM = 128
GQA_HEADS = D_MODEL // GQA_HEAD_DIM
GQA_KV_HEADS = GQA_HEADS // 4
GQA_GROUP = GQA_HEADS // GQA_KV_HEADS

LANES = 128
HEAD_V = 128
MOD_ROWS = 16
VMEM_LIMIT = 56 * 1024 * 1024
LOG2E = math.log2(math.e)
ATTN_CHUNK = 256
ATTN_SKEW = 1


def _params(*sem):
    return pltpu.CompilerParams(dimension_semantics=sem, vmem_limit_bytes=VMEM_LIMIT)


def _silu(v):
    return v * jax.nn.sigmoid(v)


_NT_DIMS = (((1,), (1,)), ((), ()))


def _dot_nt(a, b):
    return lax.dot_general(a, b, _NT_DIMS, preferred_element_type=F32)


def _mod_kernel(c_ref, w_ref, b_ref, o_ref):
    s = _silu(c_ref[...]).astype(BF16)
    o_ref[0] = jnp.dot(s, w_ref[0].astype(BF16), preferred_element_type=F32) + b_ref[0]


def _mod_vectors(cvec, w_mod, b_mod):
    depth, d, n3 = w_mod.shape
    tn = 768
    return pl.pallas_call(
        _mod_kernel,
        grid=(depth, n3 // tn),
        in_specs=[
            pl.BlockSpec((MOD_ROWS, d), lambda l, j: (0, 0)),
            pl.BlockSpec((1, d, tn), lambda l, j: (l, 0, j)),
            pl.BlockSpec((1, 1, tn), lambda l, j: (l, 0, j)),
        ],
        out_specs=pl.BlockSpec((1, MOD_ROWS, tn), lambda l, j: (l, 0, j)),
        out_shape=jax.ShapeDtypeStruct((depth, MOD_ROWS, n3), F32),
        compiler_params=_params("arbitrary", "arbitrary"),
        name="mod_vectors",
    )(cvec, w_mod, b_mod.reshape(depth, 1, n3))


def _rms_rope(y, g, cos, sin):
    ms = jnp.mean(y * y, axis=-1, keepdims=True)
    yn = y * lax.rsqrt(ms + EPS) * g
    return yn * cos + pltpu.roll(yn, LANES // 2, 1) * sin


def _modulate(x_ref, sh_ref, sc_ref, xin_ref):
    @pl.when(pl.program_id(1) == 0)
    def _():
        xin_ref[...] = (x_ref[...] * (1.0 + sc_ref[0]) + sh_ref[0]).astype(BF16)


def _plain_kernel(x_ref, sh_ref, sc_ref, w_ref, o_ref, xin_ref, *, act):
    _modulate(x_ref, sh_ref, sc_ref, xin_ref)
    half = xin_ref.shape[0] // 2
    w = w_ref[...]
    accs = [jnp.dot(xin_ref[r0:r0 + half, :], w, preferred_element_type=F32) for r0 in (0, half)]
    for r0, acc in zip((0, half), accs):
        o_ref[r0:r0 + half, :] = (_silu(acc) if act == "silu" else acc).astype(o_ref.dtype)


def _q_heads_kernel(x_ref, sh_ref, sc_ref, wt_ref, g_ref, gp_ref, cos_ref, sin_ref, qt_ref, xin_ref, *, q_scale):
    _modulate(x_ref, sh_ref, sc_ref, xin_ref)
    acc = _dot_nt(wt_ref[...], xin_ref[...])
    a = g_ref[...] * cos_ref[...]
    b = gp_ref[...] * sin_ref[...]
    half = LANES // 2
    for h in range(acc.shape[0] // LANES):
        y = acc[h * LANES:(h + 1) * LANES]
        yn = y * (lax.rsqrt(jnp.mean(y * y, axis=0, keepdims=True) + EPS) * q_scale)
        partner = jnp.concatenate([yn[half:], yn[:half]], axis=0)
        qt_ref[h * LANES:(h + 1) * LANES, :] = (yn * a + partner * b).astype(qt_ref.dtype)


def _kv_heads_kernel(x_ref, sh_ref, sc_ref, wk_ref, wvt_ref, g_ref, cos_ref, sin_ref, k_ref, vt_ref, xin_ref):
    _modulate(x_ref, sh_ref, sc_ref, xin_ref)
    xin = xin_ref[...]
    g, cos, sin = g_ref[...], cos_ref[...], sin_ref[...]
    yk = jnp.dot(xin, wk_ref[...], preferred_element_type=F32)
    for h in range(yk.shape[1] // LANES):
        k_ref[:, h * LANES:(h + 1) * LANES] = _rms_rope(
            yk[:, h * LANES:(h + 1) * LANES], g, cos, sin).astype(k_ref.dtype)
    vt_ref[...] = _dot_nt(wvt_ref[...], xin).astype(vt_ref.dtype)


def _mod_specs(d, tm, rows_per_mod, mod_row0):
    if rows_per_mod is None:
        mod_idx = lambda i: mod_row0
    else:
        assert rows_per_mod % tm == 0
        tiles_per_mod = rows_per_mod // tm
        mod_idx = lambda i: mod_row0 + i // tiles_per_mod
    return [
        pl.BlockSpec((tm, d), lambda i, j: (i, 0)),
        pl.BlockSpec((1, 1, d), lambda i, j: (mod_idx(i), 0, 0)),
        pl.BlockSpec((1, 1, d), lambda i, j: (mod_idx(i), 0, 1)),
    ]


def _mod_matmul(x, mod3, w, *, rows_per_mod, mod_row0, tm, tn, act="none"):
    m, d = x.shape
    n = w.shape[1]
    assert m % tm == 0 and n % tn == 0
    return pl.pallas_call(
        functools.partial(_plain_kernel, act=act),
        grid=(m // tm, n // tn),
        in_specs=_mod_specs(d, tm, rows_per_mod, mod_row0) + [pl.BlockSpec((d, tn), lambda i, j: (0, j))],
        out_specs=pl.BlockSpec((tm, tn), lambda i, j: (i, j)),
        out_shape=jax.ShapeDtypeStruct((m, n), BF16),
        scratch_shapes=[pltpu.VMEM((tm, d), BF16)],
        compiler_params=_params("arbitrary", "arbitrary"),
        name="mod_matmul_" + act,
    )(x, mod3, mod3, w)


def _q_heads(x, mod3, wt, gain, gain_partner, cos_t, sin_t, *, rows_per_mod, mod_row0, tm, tn, q_scale):
    m, d = x.shape
    n = wt.shape[0]
    pos = cos_t.shape[1]
    assert m % tm == 0 and n % tn == 0 and pos % tm == 0
    tiles_per_seq = pos // tm
    col = pl.BlockSpec((LANES, 1), lambda i, j: (0, 0))
    tab = pl.BlockSpec((LANES, tm), lambda i, j: (0, i % tiles_per_seq))
    return pl.pallas_call(
        functools.partial(_q_heads_kernel, q_scale=q_scale),
        grid=(m // tm, n // tn),
        in_specs=_mod_specs(d, tm, rows_per_mod, mod_row0) + [
            pl.BlockSpec((tn, d), lambda i, j: (j, 0), pipeline_mode=pl.Buffered(1 if tn == n else 2)),
            col, col, tab, tab],
        out_specs=pl.BlockSpec((tn, tm), lambda i, j: (j, i)),
        out_shape=jax.ShapeDtypeStruct((n, m), BF16),
        scratch_shapes=[pltpu.VMEM((tm, d), BF16)],
        compiler_params=_params("arbitrary", "arbitrary"),
        name="q_heads",
    )(x, mod3, mod3, wt, gain, gain_partner, cos_t, sin_t)


def _kv_heads(x, mod3, wk, wvt, gain, cos, sin, *, rows_per_mod, mod_row0, tm):
    m, d = x.shape
    nk = wk.shape[1]
    nv = wvt.shape[0]
    pos = cos.shape[0]
    assert m % tm == 0 and pos % tm == 0
    tiles_per_seq = pos // tm
    tab = pl.BlockSpec((tm, LANES), lambda i, j: (i % tiles_per_seq, 0))
    return pl.pallas_call(
        _kv_heads_kernel,
        grid=(m // tm, 1),
        in_specs=_mod_specs(d, tm, rows_per_mod, mod_row0) + [
            pl.BlockSpec((d, nk), lambda i, j: (0, 0)),
            pl.BlockSpec((nv, d), lambda i, j: (0, 0)),
            pl.BlockSpec((1, LANES), lambda i, j: (0, 0)), tab, tab],
        out_specs=[pl.BlockSpec((tm, nk), lambda i, j: (i, 0)), pl.BlockSpec((nv, tm), lambda i, j: (0, i))],
        out_shape=[jax.ShapeDtypeStruct((m, nk), BF16), jax.ShapeDtypeStruct((nv, m), BF16)],
        scratch_shapes=[pltpu.VMEM((tm, d), BF16)],
        compiler_params=_params("arbitrary", "arbitrary"),
        name="kv_heads",
    )(x, mod3, mod3, wk, wvt, gain, cos, sin)


def _mla_proj_kernel(ql_ref, kvl_ref, kr_ref, qg_ref, kvg_ref, wqt_ref, wk_ref, wvt_ref, cs_ref, cst_ref,
                     qt_ref, k_ref, vt_ref, *, q_scale):
    def rms(v, g):
        vf = v.astype(F32)
        return (vf * lax.rsqrt(jnp.mean(vf * vf, axis=-1, keepdims=True) + EPS) * g).astype(BF16)

    yqt = _dot_nt(wqt_ref[...], rms(ql_ref[...], qg_ref[...]))
    cst = cst_ref[...]
    for h in range(MLA_HEADS):
        lo = h * MLA_QK_PAD
        qt_ref[lo:lo + MLA_NOPE, :] = (yqt[lo:lo + MLA_NOPE] * q_scale).astype(qt_ref.dtype)
        t = yqt[lo + MLA_NOPE:lo + MLA_QK_PAD] * cst
        rope = (t[:MLA_ROPE] + t[MLA_ROPE:]) * q_scale
        qt_ref[lo + MLA_NOPE:lo + MLA_NOPE + MLA_ROPE, :] = rope.astype(qt_ref.dtype)
        qt_ref[lo + MLA_NOPE + MLA_ROPE:lo + MLA_QK_PAD, :] = jnp.zeros((MLA_ROPE, rope.shape[1]), qt_ref.dtype)

    rkv = rms(kvl_ref[...], kvg_ref[...])
    yk = jnp.dot(rkv, wk_ref[...], preferred_element_type=F32)
    t = kr_ref[...].astype(F32) * cs_ref[...]
    kr = t + pltpu.roll(t, LANES // 2, 1)
    lane = lax.broadcasted_iota(jnp.int32, kr.shape, 1)
    kr = jnp.where(lane < MLA_ROPE, kr, 0.0).astype(k_ref.dtype)
    for h in range(MLA_HEADS):
        lo = h * MLA_QK_PAD
        k_ref[:, lo:lo + MLA_NOPE] = yk[:, h * MLA_NOPE:(h + 1) * MLA_NOPE].astype(k_ref.dtype)
        k_ref[:, lo + MLA_NOPE:lo + MLA_QK_PAD] = kr
    vt_ref[...] = _dot_nt(wvt_ref[...], rkv).astype(vt_ref.dtype)


def _mla_proj(p_lat, q_gain, kv_gain, wqt, wk, wvt, cs_tab, cs_tab_t, *, tm, q_scale):
    m = p_lat.shape[0]
    pos_rows = cs_tab.shape[0]
    assert m % tm == 0 and pos_rows % tm == 0
    tiles_per_seq = pos_rows // tm
    hq = MLA_HEADS * MLA_QK_PAD
    hv = MLA_HEADS * MLA_V
    rope_blk = (MLA_Q_RANK + MLA_KV_RANK) // LANES
    return pl.pallas_call(
        functools.partial(_mla_proj_kernel, q_scale=q_scale),
        grid=(m // tm,),
        in_specs=[
            pl.BlockSpec((tm, MLA_Q_RANK), lambda i: (i, 0)),
            pl.BlockSpec((tm, MLA_KV_RANK), lambda i: (i, 1)),
            pl.BlockSpec((tm, LANES), lambda i: (i, rope_blk)),
            pl.BlockSpec((1, MLA_Q_RANK), lambda i: (0, 0)),
            pl.BlockSpec((1, MLA_KV_RANK), lambda i: (0, 0)),
            pl.BlockSpec((hq, MLA_Q_RANK), lambda i: (0, 0)),
            pl.BlockSpec((MLA_KV_RANK, hv), lambda i: (0, 0)),
            pl.BlockSpec((hv, MLA_KV_RANK), lambda i: (0, 0)),
            pl.BlockSpec((tm, LANES), lambda i: (i % tiles_per_seq, 0)),
            pl.BlockSpec((LANES, tm), lambda i: (0, i % tiles_per_seq)),
        ],
        out_specs=[
            pl.BlockSpec((hq, tm), lambda i: (0, i)),
            pl.BlockSpec((tm, hq), lambda i: (i, 0)),
            pl.BlockSpec((hv, tm), lambda i: (0, i)),
        ],
        out_shape=[
            jax.ShapeDtypeStruct((hq, m), BF16),
            jax.ShapeDtypeStruct((m, hq), BF16),
            jax.ShapeDtypeStruct((hv, m), BF16),
        ],
        compiler_params=_params("arbitrary"),
        name="mla_proj",
    )(p_lat, p_lat, p_lat, q_gain, kv_gain, wqt, wk, wvt, cs_tab, cs_tab_t)


def _attn_kernel(*refs, sub, heads, group, dk, lat_keys, ctx_keys, stages_per_trip):
    if lat_keys:
        qt_ref, kl_ref, vtl_ref, kc_ref, vtc_ref, o_ref, s_a, s_b = refs
    else:
        qt_ref, kc_ref, vtc_ref, o_ref, s_a, s_b = refs
        kl_ref = vtl_ref = None
    n_sub = qt_ref.shape[1] // sub
    n_elems = heads * n_sub
    pieces = [(kl_ref, vtl_ref, c * ATTN_CHUNK, c * ATTN_CHUNK, ATTN_CHUNK) for c in range(lat_keys // ATTN_CHUNK)]
    pieces.append((kc_ref, vtc_ref, 0, lat_keys, ctx_keys))

    def elem(t):
        if isinstance(t, int):
            h, j = t // n_sub, t % n_sub
            return h, h // group, j, (s_a, s_b)[t % 2]
        assert n_sub & (n_sub - 1) == 0 and group & (group - 1) == 0
        h = lax.shift_right_logical(t, n_sub.bit_length() - 1)
        return h, lax.shift_right_logical(h, group.bit_length() - 1), lax.bitwise_and(t, n_sub - 1), None

    def fold8(v):
        return v.reshape(v.shape[0] // 8, 8, v.shape[1])

    def scores(slot, qt, k_rows, row0, rows):
        s = jnp.dot(k_rows, qt, preferred_element_type=F32)
        slot[row0:row0 + rows, :] = s
        return jnp.max(fold8(s), axis=0)

    def weighted(slot, m, vt_cols, row0, rows):
        p = jnp.exp2(slot[row0:row0 + rows, :] - m)
        return jnp.dot(vt_cols, p.astype(BF16), preferred_element_type=F32), jnp.sum(fold8(p), axis=0)

    def stage(cur, prev, m2):
        mrun = jnp.full((8, sub), -jnp.inf, F32)
        lrun = jnp.zeros((8, sub), F32)
        acc = None
        if cur is not None:
            h1, hk1, j1, slot1 = cur
            qt = qt_ref[pl.ds(h1 * dk, dk), pl.ds(j1 * sub, sub)]
        if prev is not None:
            h2, hk2, j2, slot2 = prev
        lead = ATTN_SKEW if (cur is not None and prev is not None) else 0
        for c in range(len(pieces) + lead):
            if cur is not None and c < len(pieces):
                k_ref, _, src0, dst0, rows = pieces[c]
                cm = scores(slot1, qt, k_ref[src0:src0 + rows, pl.ds(hk1 * dk, dk)], dst0, rows)
                mrun = jnp.maximum(mrun, cm)
            if prev is not None and c >= lead:
                _, vt_ref, src0, dst0, rows = pieces[c - lead]
                d, lsum = weighted(slot2, m2, vt_ref[pl.ds(hk2 * HEAD_V, HEAD_V), src0:src0 + rows], dst0, rows)
                acc = d if acc is None else acc + d
                lrun = lrun + lsum
        if prev is not None:
            ot = acc / jnp.sum(lrun, axis=0, keepdims=True)
            o_ref[pl.ds(j2 * sub, sub), pl.ds(h2 * HEAD_V, HEAD_V)] = ot.T.astype(o_ref.dtype)
        return jnp.max(mrun, axis=0, keepdims=True) if cur is not None else None

    m = stage(elem(0), None, None)

    assert stages_per_trip % 2 == 0

    def stage_group(i, m):
        t0 = stages_per_trip * i
        prev = elem(t0)[:3] + (s_a,)
        for d in range(1, stages_per_trip + 1):
            cur = elem(t0 + d)[:3] + ((s_a, s_b)[d % 2],)
            m = stage(cur, prev, m)
            prev = cur
        return m

    n_trips = (n_elems - 1) // stages_per_trip
    if n_trips:
        m = lax.fori_loop(0, n_trips, stage_group, m)
    for t in range(stages_per_trip * n_trips + 1, n_elems):
        m = stage(elem(t), elem(t - 1), m)
    stage(None, elem(n_elems - 1), m)


def _attention(qt, kv_lat, kv_ctx, *, batch, heads, heads_per_step, group, dk, sub, lat_keys, ctx_keys,
               stages_per_trip):
    mq = qt.shape[1]
    nq = mq // batch
    hs = heads_per_step
    assert nq % sub == 0 and heads % hs == 0 and (hs % group == 0 or group % hs == 0)
    kvs = max(1, hs // group)
    kv_blk = lambda g: g * hs // (group * kvs)
    in_specs = [pl.BlockSpec((hs * dk, nq), lambda b, g: (g, b))]
    operands = [qt]
    n_keys = ctx_keys
    if kv_lat is not None:
        assert lat_keys % ATTN_CHUNK == 0
        n_keys += lat_keys
        in_specs += [
            pl.BlockSpec((lat_keys, kvs * dk), lambda b, g: (b, kv_blk(g))),
            pl.BlockSpec((kvs * HEAD_V, lat_keys), lambda b, g: (kv_blk(g), b)),
        ]
        operands += list(kv_lat)
    in_specs += [
        pl.BlockSpec((ctx_keys, kvs * dk), lambda b, g: (b, kv_blk(g))),
        pl.BlockSpec((kvs * HEAD_V, ctx_keys), lambda b, g: (kv_blk(g), b)),
    ]
    operands += list(kv_ctx)
    return pl.pallas_call(
        functools.partial(_attn_kernel, sub=sub, heads=hs, group=min(group, hs), dk=dk,
                          lat_keys=lat_keys if kv_lat is not None else 0, ctx_keys=ctx_keys,
                          stages_per_trip=stages_per_trip),
        grid=(batch, heads // hs),
        in_specs=in_specs,
        out_specs=pl.BlockSpec((nq, hs * HEAD_V), lambda b, g: (b, g)),
        out_shape=jax.ShapeDtypeStruct((mq, heads * HEAD_V), BF16),
        scratch_shapes=[pltpu.VMEM((n_keys, sub), F32), pltpu.VMEM((n_keys, sub), F32)],
        compiler_params=_params("arbitrary", "arbitrary"),
        name="attention",
    )(*operands)


def _conv_kernel(cb_ref, cc_ref, cu_ref, w_ref, o_ref):
    t = cc_ref[...].astype(F32) * cu_ref[...].astype(F32)
    n = t.shape[0]
    row = lax.broadcasted_iota(jnp.int32, t.shape, 0)
    t_prev = jnp.where(row == 0, 0.0, pltpu.roll(t, 1, 0))
    t_next = jnp.where(row == n - 1, 0.0, pltpu.roll(t, n - 1, 0))
    w = w_ref[...]
    conv = t_prev * w[0:1, :] + t * w[1:2, :] + t_next * w[2:3, :]
    o_ref[...] = (cb_ref[...].astype(F32) * conv).astype(o_ref.dtype)


def _short_conv_gate(p_conv, conv_w, *, batch, width):
    m = p_conv.shape[0]
    n = m // batch
    cblk = CONV_WIDTH // width
    return pl.pallas_call(
        _conv_kernel,
        grid=(batch, cblk),
        in_specs=[
            pl.BlockSpec((n, width), lambda b, j: (b, j)),
            pl.BlockSpec((n, width), lambda b, j: (b, cblk + j)),
            pl.BlockSpec((n, width), lambda b, j: (b, 2 * cblk + j)),
            pl.BlockSpec((3, width), lambda b, j: (0, j)),
        ],
        out_specs=pl.BlockSpec((n, width), lambda b, j: (b, j)),
        out_shape=jax.ShapeDtypeStruct((m, CONV_WIDTH), BF16),
        compiler_params=_params("arbitrary", "arbitrary"),
        name="short_conv",
    )(p_conv, p_conv, p_conv, conv_w)


def _out_kernel(*refs, n_parts):
    parts = refs[:n_parts]
    g_ref, w_ref, x_ref, gate_ref, lng_ref, lnb_ref, o_ref = refs[n_parts:]
    half = x_ref.shape[0] // 2
    ys = []
    for r0 in (0, half):
        y = None
        off = 0
        for p_ref in parts:
            kp = p_ref.shape[1]
            u = p_ref[r0:r0 + half, :] * g_ref[r0:r0 + half, off:off + kp]
            part = jnp.dot(u, w_ref[off:off + kp, :], preferred_element_type=F32)
            y = part if y is None else y + part
            off += kp
        ys.append(y)
    for r0, y in zip((0, half), ys):
        z = ALPHA * x_ref[r0:r0 + half, :] + gate_ref[0] * y
        mu = jnp.mean(z, axis=-1, keepdims=True)
        zc = z - mu
        var = jnp.mean(zc * zc, axis=-1, keepdims=True)
        o_ref[r0:r0 + half, :] = zc * lax.rsqrt(var + EPS) * lng_ref[...] + lnb_ref[...]


def _out_proj_norm(parts, g, w, x, mod3, ln_g, ln_b, *, rows_per_mod, mod_row0, tm):
    m, d = x.shape
    assert m % tm == 0
    if rows_per_mod is None:
        mod_idx = lambda i: mod_row0
    else:
        tiles_per_mod = rows_per_mod // tm
        mod_idx = lambda i: mod_row0 + i // tiles_per_mod
    in_specs = [pl.BlockSpec((tm, p.shape[1]), lambda i: (i, 0)) for p in parts]
    in_specs += [
        pl.BlockSpec((tm, g.shape[1]), lambda i: (i, 0)),
        pl.BlockSpec(w.shape, lambda i: (0, 0), pipeline_mode=pl.Buffered(1)),
        pl.BlockSpec((tm, d), lambda i: (i, 0)),
        pl.BlockSpec((1, 1, d), lambda i: (mod_idx(i), 0, 2)),
        pl.BlockSpec((1, d), lambda i: (0, 0)),
        pl.BlockSpec((1, d), lambda i: (0, 0)),
    ]
    return pl.pallas_call(
        functools.partial(_out_kernel, n_parts=len(parts)),
        grid=(m // tm,),
        in_specs=in_specs,
        out_specs=pl.BlockSpec((tm, d), lambda i: (i, 0)),
        out_shape=jax.ShapeDtypeStruct((m, d), F32),
        compiler_params=_params("arbitrary"),
        name="out_proj_norm",
    )(*parts, g, w, x, mod3, ln_g, ln_b)


def _rope_tables(n, ctx_len):
    rows = n // GRID_W
    pos_row = np.repeat(np.arange(rows, dtype=np.float32), GRID_W)
    pos_col = np.tile(np.arange(GRID_W, dtype=np.float32), rows)

    def angles(pos, half):
        inv = np.float32(ROPE_THETA) ** (-np.arange(half, dtype=np.float32) / np.float32(half))
        return (pos[:, None] * inv[None, :]).astype(np.float32)

    f32 = np.float32
    ar, ac = angles(pos_row, MLA_ROPE // 4), angles(pos_col, MLA_ROPE // 4)
    cs_mla = np.concatenate([np.cos(ar), np.cos(ar), np.cos(ac), np.cos(ac),
                             -np.sin(ar), np.sin(ar), -np.sin(ac), np.sin(ac)], axis=1).astype(f32)
    cs_mla_ctx = np.concatenate([np.ones((ctx_len, MLA_ROPE), f32), np.zeros((ctx_len, MLA_ROPE), f32)], axis=1)
    br, bc = angles(pos_row, GQA_HEAD_DIM // 4), angles(pos_col, GQA_HEAD_DIM // 4)
    cos_gqa = np.concatenate([np.cos(br), np.cos(bc), np.cos(br), np.cos(bc)], axis=1).astype(f32)
    sin_gqa = np.concatenate([-np.sin(br), -np.sin(bc), np.sin(br), np.sin(bc)], axis=1).astype(f32)
    cos_ctx = np.ones((ctx_len, GQA_HEAD_DIM), f32)
    sin_ctx = np.zeros((ctx_len, GQA_HEAD_DIM), f32)
    tabs = dict(mla=cs_mla, mla_ctx=cs_mla_ctx, mla_t=cs_mla.T, mla_ctx_t=cs_mla_ctx.T,
                cos=cos_gqa, sin=sin_gqa, cos_t=cos_gqa.T, sin_t=sin_gqa.T, cos_ctx=cos_ctx, sin_ctx=sin_ctx)
    return {k: jnp.asarray(np.ascontiguousarray(v)) for k, v in tabs.items()}


def kernel(x, c, ctx, c_ctx, w_mod, b_mod, ln_g, ln_b, a_w_in, a_conv_w, a_q_norm, a_w_qb, a_kv_norm,
           a_w_kvb, a_w_out, c_w_in, c_q_norm, c_k_norm, c_w_out):
    batch, n, d = x.shape
    ctx_len = ctx.shape[1]
    assert d == D_MODEL and batch + 1 <= MOD_ROWS
    ctx_mod_row = batch

    tabs = _rope_tables(n, ctx_len)

    cvec = jnp.concatenate([c, c_ctx[None, :], jnp.zeros((MOD_ROWS - batch - 1, d), F32)], axis=0)
    mods = _mod_vectors(cvec, w_mod, b_mod)
    mod0 = mods[0].reshape(MOD_ROWS, 1, 3 * d)
    mod1 = mods[1].reshape(MOD_ROWS, 1, 3 * d)

    xl = x.reshape(batch * n, d)
    xc = ctx.reshape(batch * ctx_len, d)

    w_in0 = a_w_in[0]
    o_c = 3 * CONV_WIDTH
    o_q = o_c + MLA_Q_RANK
    o_kv = o_q + MLA_KV_RANK
    o_g = o_kv + MLA_ROPE
    swap16 = np.concatenate([np.arange(16, 32), np.arange(0, 16), np.arange(48, 64), np.arange(32, 48)])
    w_conv = w_in0[:, :o_c].astype(BF16)
    w_rope = w_in0[:, o_kv:o_g]
    w_lat = jnp.concatenate([w_in0[:, o_c:o_kv], w_rope, w_rope[:, swap16]], axis=1).astype(BF16)
    w_gate0 = w_in0[:, o_g:].astype(BF16)
    wq3 = a_w_qb[0].reshape(MLA_Q_RANK, MLA_HEADS, MLA_NOPE + MLA_ROPE)
    wq_rope = wq3[:, :, MLA_NOPE:]
    wq = jnp.concatenate([wq3[:, :, :MLA_NOPE], wq_rope, wq_rope[:, :, swap16]], axis=2)
    wq_t = wq.reshape(MLA_Q_RANK, MLA_HEADS * MLA_QK_PAD).T.astype(BF16)
    wkv3 = a_w_kvb[0].reshape(MLA_KV_RANK, MLA_HEADS, MLA_NOPE + MLA_V)
    wk0 = wkv3[:, :, :MLA_NOPE].reshape(MLA_KV_RANK, -1).astype(BF16)
    wv0_t = wkv3[:, :, MLA_NOPE:].reshape(MLA_KV_RANK, -1).T.astype(BF16)
    w_out0 = a_w_out[0].astype(BF16)

    w_in1 = c_w_in[0]
    nq1 = GQA_HEADS * GQA_HEAD_DIM
    nk1 = GQA_KV_HEADS * GQA_HEAD_DIM
    perm = np.concatenate([np.arange(0, 32), np.arange(64, 96), np.arange(32, 64), np.arange(96, 128)])
    partner = np.concatenate([np.arange(64, 128), np.arange(0, 64)])
    w_q1_t = w_in1[:, :nq1].reshape(d, GQA_HEADS, GQA_HEAD_DIM)[:, :, perm].reshape(d, nq1).T.astype(BF16)
    w_k1 = w_in1[:, nq1:nq1 + nk1].reshape(d, GQA_KV_HEADS, GQA_HEAD_DIM)[:, :, perm].reshape(d, nk1).astype(BF16)
    w_v1_t = w_in1[:, nq1 + nk1:nq1 + 2 * nk1].T.astype(BF16)
    w_gate1 = w_in1[:, nq1 + 2 * nk1:].astype(BF16)
    q_gain1 = c_q_norm[0][perm].reshape(GQA_HEAD_DIM, 1)
    q_gain1_partner = c_q_norm[0][perm][partner].reshape(GQA_HEAD_DIM, 1)
    k_gain1 = c_k_norm[0][perm].reshape(1, GQA_HEAD_DIM)
    w_out1 = c_w_out[0].astype(BF16)

    lng0, lnb0 = ln_g[0].reshape(1, d), ln_b[0].reshape(1, d)
    lng1, lnb1 = ln_g[1].reshape(1, d), ln_b[1].reshape(1, d)

    lat = dict(rows_per_mod=n, mod_row0=0)
    cxt = dict(rows_per_mod=None, mod_row0=ctx_mod_row)
    tm_l, tm_c = 1024, ctx_len
    q_scale0 = (MLA_NOPE + MLA_ROPE) ** -0.5 * LOG2E
    q_scale1 = GQA_HEAD_DIM ** -0.5 * LOG2E

    def layer0_front(rows, mod_kw, tm_proj, cs_tab, cs_tab_t):
        p_conv = _mod_matmul(rows, mod0, w_conv, tm=tm_l, tn=1024, **mod_kw)
        p_lat = _mod_matmul(rows, mod0, w_lat, tm=tm_l, tn=w_lat.shape[1], **mod_kw)
        gate = _mod_matmul(rows, mod0, w_gate0, tm=tm_l, tn=1024, act="silu", **mod_kw)
        qt, k, vt = _mla_proj(p_lat, a_q_norm[0].reshape(1, -1), a_kv_norm[0].reshape(1, -1), wq_t, wk0, wv0_t,
                              cs_tab, cs_tab_t, tm=tm_proj, q_scale=q_scale0)
        return p_conv, gate, qt, k, vt

    pcl, gl, qtl, kl, vtl = layer0_front(xl, lat, 512, tabs["mla"], tabs["mla_t"])
    pcc, gc, qtc, kc, vtc = layer0_front(xc, cxt, tm_c, tabs["mla_ctx"], tabs["mla_ctx_t"])

    mla = dict(batch=batch, heads=MLA_HEADS, group=1, dk=MLA_QK_PAD, lat_keys=n, ctx_keys=ctx_len)
    attn_l = _attention(qtl, (kl, vtl), (kc, vtc), heads_per_step=2, sub=512, stages_per_trip=4, **mla)
    attn_c = _attention(qtc, None, (kc, vtc), heads_per_step=MLA_HEADS, sub=ctx_len, stages_per_trip=2, **mla)

    conv_l = _short_conv_gate(pcl, a_conv_w[0], batch=batch, width=2 * LANES)
    conv_c = _short_conv_gate(pcc, a_conv_w[0], batch=batch, width=CONV_WIDTH)

    x1 = _out_proj_norm([conv_l, attn_l], gl, w_out0, xl, mod0, lng0, lnb0, tm=512, **lat)
    h1 = _out_proj_norm([conv_c, attn_c], gc, w_out0, xc, mod0, lng0, lnb0, tm=256, **cxt)

    qt1 = _q_heads(x1, mod1, w_q1_t, q_gain1, q_gain1_partner, tabs["cos_t"], tabs["sin_t"],
                   tm=tm_l, tn=nq1, q_scale=q_scale1, **lat)
    k1, vt1 = _kv_heads(x1, mod1, w_k1, w_v1_t, k_gain1, tabs["cos"], tabs["sin"], tm=512, **lat)
    g1 = _mod_matmul(x1, mod1, w_gate1, tm=tm_l, tn=1024, act="silu", **lat)
    kc1, vtc1 = _kv_heads(h1, mod1, w_k1, w_v1_t, k_gain1, tabs["cos_ctx"], tabs["sin_ctx"], tm=tm_c, **cxt)

    attn1 = _attention(qt1, (k1, vt1), (kc1, vtc1), batch=batch, heads=GQA_HEADS, heads_per_step=GQA_GROUP,
                       group=GQA_GROUP, dk=GQA_HEAD_DIM, sub=512, lat_keys=n, ctx_keys=ctx_len, stages_per_trip=6)

    out = _out_proj_norm([attn1], g1, w_out1, x1, mod1, lng1, lnb1, tm=512, **lat)
    return out.reshape(batch, n, d)
```
